```python
import jax
import jax.numpy as jnp
from jax import lax
import numpy as np

D_MODEL = 4096
BATCH = 4
SEQ = 2048
DEPTH = 4
DEC_BATCH = 128
DEC_SEQ = 1
PAST_LEN = 16384
PAGE_SIZE = 128

HG_HEADS = 8
HG_DK = 128
HG_DV = 128
HG_W = HG_HEADS * HG_DK
GDN_HEADS = 8
GDN_DK = 128
GDN_DV = 128
GDN_W = GDN_HEADS * GDN_DK
M2_INNER = D_MODEL // 2
M2_HEAD_DIM = 64
M2_HEADS = M2_INNER // M2_HEAD_DIM
M2_GROUPS = 4
M2_STATE = 128
M2_CONV_DIM = M2_INNER + 2 * M2_GROUPS * M2_STATE
CONV_W = 4
CHUNK = 64
MIX_W = HG_W + GDN_W + M2_INNER
D_FF = -(-8 * D_MODEL // (3 * 256)) * 256
EPS = 1e-6
IN_SIZES = (HG_W, HG_W, HG_W, HG_W, 3 * GDN_W, GDN_HEADS, GDN_HEADS, GDN_W,
            M2_INNER, M2_CONV_DIM, M2_HEADS, D_MODEL, D_MODEL, D_MODEL)
N_IN = sum(IN_SIZES)

kernel_name = 'hybrid_hgrn2_gdn_mamba2_step'


def _split_points():
    pts, acc = [], 0
    for s in IN_SIZES[:-1]:
        acc += s
        pts.append(acc)
    return pts


def rmsnorm(x, w):
    xf = x.astype(jnp.float32)
    y = xf * lax.rsqrt(jnp.mean(xf * xf, axis=-1, keepdims=True) + EPS) * w.astype(jnp.float32)
    return y.astype(x.dtype)


def l2norm(t):
    return t * lax.rsqrt(jnp.sum(t * t, axis=-1, keepdims=True) + EPS)


def causal_conv(x, buf, w, b=None):
    L = x.shape[1]
    xp = jnp.concatenate([buf.astype(x.dtype), x], axis=1)
    xf = xp.astype(jnp.float32)
    wf = w.astype(jnp.float32)
    out = xf[:, 0:L] * wf[0]
    for j in range(1, CONV_W):
        out = out + xf[:, j:j + L] * wf[j]
    if b is not None:
        out = out + b.astype(jnp.float32)
    return out, xp[:, -(CONV_W - 1):]


def _pad_time(t, Lp):
    pad = [(0, 0)] * t.ndim
    pad[1] = (0, Lp - t.shape[1])
    return jnp.pad(t, pad)


def _to_chunks(t, C):
    B, L = t.shape[:2]
    t = t.reshape((B, L // C, C) + t.shape[2:])
    return jnp.moveaxis(t, 1, 0)


def chunked_scan(step, seqs, S0):
    L = seqs[0].shape[1]
    C = min(CHUNK, L)
    Lp = -(-L // C) * C
    chunks = tuple(_to_chunks(_pad_time(t.astype(jnp.float32), Lp), C) for t in seqs)
    S, out = lax.scan(step, S0.astype(jnp.float32), chunks)
    out = jnp.moveaxis(out, 0, 1)
    out = out.reshape((out.shape[0], Lp) + out.shape[3:])[:, :L]
    return out, S


def _decay_matrix(G):
    C = G.shape[1]
    Gh = jnp.moveaxis(G, 1, 2)
    mask = jnp.tril(jnp.ones((C, C), bool))
    return jnp.exp(jnp.where(mask, Gh[..., :, None] - Gh[..., None, :], -jnp.inf))


def _hgrn2_step(S, inp):
    q, k, v, g = inp
    C = q.shape[1]
    G = jnp.cumsum(g, axis=1)
    mask = jnp.tril(jnp.ones((C, C), bool))[None, :, :, None, None]
    decay = jnp.exp(jnp.where(mask, G[:, :, None] - G[:, None, :], -jnp.inf))
    A = jnp.sum(q[:, :, None] * k[:, None, :] * decay, axis=-1)
    o = jnp.einsum('bijh,bjhv->bihv', A, v) + jnp.einsum('bihk,bhkv->bihv', q * jnp.exp(G), S)
    G_last = G[:, -1]
    S = S * jnp.exp(G_last)[..., None] + jnp.einsum('bjhk,bjhv->bhkv', k * jnp.exp(G_last[:, None] - G), v)
    return S, o


def _gdn_step(S, inp):
    q, k, v, beta, g = inp
    C = q.shape[1]
    G = jnp.cumsum(g, axis=1)
    decay = _decay_matrix(G)
    strict = jnp.tril(jnp.ones((C, C), bool), -1)
    kb = k * beta[..., None]
    KK = jnp.where(strict, jnp.einsum('bihk,bjhk->bhij', kb, k) * decay, 0.0)
    T = KK + jnp.eye(C, dtype=KK.dtype)
    eG = jnp.exp(G)
    rhs_v = jnp.einsum('bchv->bhcv', v * beta[..., None])
    rhs_k = jnp.einsum('bchk->bhck', kb * eG[..., None])
    U = lax.linalg.triangular_solve(T, rhs_v, left_side=True, lower=True, unit_diagonal=True)
    Wk = lax.linalg.triangular_solve(T, rhs_k, left_side=True, lower=True, unit_diagonal=True)
    Vn = U - jnp.einsum('bhck,bhkv->bhcv', Wk, S)
    Aqk = jnp.einsum('bihk,bjhk->bhij', q, k) * decay
    o = jnp.einsum('bihk,bhkv->bihv', q * eG[..., None], S) + jnp.einsum('bhij,bhjv->bihv', Aqk, Vn)
    G_last = G[:, -1]
    w = jnp.exp(G_last[:, None] - G)
    S = S * jnp.exp(G_last)[..., None, None] + jnp.einsum('bjhk,bhjv->bhkv', k * w[..., None], Vn)
    return S, o


def _ssd_step(S, inp):
    x, dt, a, Bm, Cm = inp
    Bsz, C, H, P = x.shape
    Gn = Bm.shape[2]
    R = H // Gn
    N = Bm.shape[3]
    G = jnp.cumsum(a, axis=1)
    decay = _decay_matrix(G).reshape(Bsz, Gn, R, C, C)
    xdt = (x * dt[..., None]).reshape(Bsz, C, Gn, R, P)
    CB = jnp.einsum('bign,bjgn->bgij', Cm, Bm)
    y = jnp.einsum('bgrij,bjgrp->bigrp', CB[:, :, None] * decay, xdt)
    Sg = S.reshape(Bsz, Gn, R, P, N)
    eG = jnp.exp(G).reshape(Bsz, C, Gn, R)
    y = y + jnp.einsum('bign,bgrpn->bigrp', Cm, Sg) * eG[..., None]
    G_last = G[:, -1]
    w = jnp.exp(G_last[:, None] - G).reshape(Bsz, C, Gn, R)
    Sg = Sg * jnp.exp(G_last).reshape(Bsz, Gn, R)[..., None, None] + \
        jnp.einsum('bjgn,bjgrp->bgrpn', Bm, xdt * w[..., None])
    return Sg.reshape(Bsz, H, P, N), y.reshape(Bsz, C, H, P)


def hgrn2_branch(q, f, i, g, lb, onorm_w, S0):
    B, L, _ = q.shape
    q = jax.nn.silu(q.astype(jnp.float32)).reshape(B, L, HG_HEADS, HG_DK)
    z = f.astype(jnp.float32).reshape(B, L, HG_HEADS, HG_DK)
    lb = lb.reshape(HG_HEADS, HG_DK)
    logf = jnp.logaddexp(jnp.log(lb), jnp.log1p(-lb) + jax.nn.log_sigmoid(z))
    k = (1.0 - lb) * jax.nn.sigmoid(-z)
    v = i.astype(jnp.float32).reshape(B, L, HG_HEADS, HG_DV)
    o, S = chunked_scan(_hgrn2_step, (q, k, v, logf), S0)
    gate = jax.nn.silu(g.astype(jnp.float32)).reshape(B, L, HG_HEADS, HG_DV)
    o = rmsnorm(o, onorm_w) * gate
    return o.reshape(B, L, HG_W), S


def gdn_branch(qkv, b, a, g, conv_w, A_log, dt_bias, onorm_w, S0, buf):
    B, L, _ = qkv.shape
    c, new_buf = causal_conv(qkv, buf, conv_w)
    c = jax.nn.silu(c)
    q, k, v = jnp.split(c, [GDN_W, 2 * GDN_W], axis=-1)
    q = l2norm(q.reshape(B, L, GDN_HEADS, GDN_DK)) * (GDN_DK ** -0.5)
    k = l2norm(k.reshape(B, L, GDN_HEADS, GDN_DK))
    v = v.reshape(B, L, GDN_HEADS, GDN_DV)
    beta = jax.nn.sigmoid(b.astype(jnp.float32))
    logd = -jnp.exp(A_log.astype(jnp.float32)) * jax.nn.softplus(a.astype(jnp.float32) + dt_bias.astype(jnp.float32))
    o, S = chunked_scan(_gdn_step, (q, k, v, beta, logd), S0)
    gate = jax.nn.silu(g.astype(jnp.float32)).reshape(B, L, GDN_HEADS, GDN_DV)
    o = rmsnorm(o, onorm_w) * gate
    return o.reshape(B, L, GDN_W), S, new_buf


def mamba2_branch(z, xbc, dt, conv_w, conv_b, dt_bias, A_log, Dskip, norm_w, S0, buf):
    B, L, _ = z.shape
    c, new_buf = causal_conv(xbc, buf, conv_w, conv_b)
    c = jax.nn.silu(c)
    xs, Bm, Cm = jnp.split(c, [M2_INNER, M2_INNER + M2_GROUPS * M2_STATE], axis=-1)
    xs = xs.reshape(B, L, M2_HEADS, M2_HEAD_DIM)
    Bm = Bm.reshape(B, L, M2_GROUPS, M2_STATE)
    Cm = Cm.reshape(B, L, M2_GROUPS, M2_STATE)
    dt = jax.nn.softplus(dt.astype(jnp.float32) + dt_bias.astype(jnp.float32))
    A = -jnp.exp(A_log.astype(jnp.float32))
    y, S = chunked_scan(_ssd_step, (xs, dt, dt * A, Bm, Cm), S0)
    y = y + xs * Dskip.astype(jnp.float32)[:, None]
    y = y.reshape(B, L, M2_INNER) * jax.nn.silu(z.astype(jnp.float32))
    gs = M2_INNER // M2_GROUPS
    y = rmsnorm(y.reshape(B, L, M2_GROUPS, gs), norm_w.reshape(M2_GROUPS, gs))
    return y.reshape(B, L, M2_INNER), S, new_buf


def run_trunk(x, st_hg, st_gdn, st_gdn_conv, st_ssm, st_ssm_conv, w):
    dt_x = x.dtype
    lb_all = jnp.cumsum(jax.nn.softmax(w['hg_lb'].astype(jnp.float32), axis=0), axis=0)
    lb_all = lb_all - lb_all[0]
    pts = _split_points()
    n_hg, n_gdn, n_gc, n_ssm, n_sc = [], [], [], [], []
    for l in range(DEPTH):
        h = rmsnorm(x, w['mix_norm'][l])
        proj = h @ w['w_in'][l]
        (hq, hf, hi, hg, gqkv, gb, ga, gg, mz, mxbc, mdt, g_a, g_b, g_c) = jnp.split(proj, pts, axis=-1)
        oa, s1 = hgrn2_branch(hq, hf, hi, hg, lb_all[l], w['hg_onorm'][l], st_hg[l])
        ob, s2, c2 = gdn_branch(gqkv, gb, ga, gg, w['gdn_conv'][l], w['gdn_A_log'][l], w['gdn_dt_bias'][l],
                                w['gdn_onorm'][l], st_gdn[l], st_gdn_conv[l])
        oc, s3, c3 = mamba2_branch(mz, mxbc, mdt, w['m2_conv_w'][l], w['m2_conv_b'][l], w['m2_dt_bias'][l],
                                   w['m2_A_log'][l], w['m2_D'][l], w['m2_norm'][l], st_ssm[l], st_ssm_conv[l])
        wb = w['w_branch'][l]
        merged = (jax.nn.sigmoid(g_a.astype(jnp.float32)).astype(dt_x) * (oa.astype(dt_x) @ wb[:HG_W])
                  + jax.nn.sigmoid(g_b.astype(jnp.float32)).astype(dt_x) * (ob.astype(dt_x) @ wb[HG_W:HG_W + GDN_W])
                  + jax.nn.sigmoid(g_c.astype(jnp.float32)).astype(dt_x) * (oc.astype(dt_x) @ wb[HG_W + GDN_W:]))
        x = x + merged @ w['w_out'][l]
        h = rmsnorm(x, w['ffn_norm'][l])
        gt, up = jnp.split(h @ w['w_ffn_in'][l], [D_FF], axis=-1)
        x = x + (jax.nn.silu(gt) * up) @ w['w_ffn_out'][l]
        n_hg.append(s1)
        n_gdn.append(s2)
        n_gc.append(c2)
        n_ssm.append(s3)
        n_sc.append(c3)
    y = rmsnorm(x, w['final_norm'])
    return y, jnp.stack(n_hg), jnp.stack(n_gdn), jnp.stack(n_gc), jnp.stack(n_ssm), jnp.stack(n_sc)


def setup_inputs(seed: int = 0) -> dict:
    key = jax.random.key(seed)
    ks = iter(jax.random.split(key, 40))
    f32 = jnp.float32

    def nrm(shape, scale):
        return scale * jax.random.normal(next(ks), shape, f32)

    def inv_softplus_dt(shape):
        u = jax.random.uniform(next(ks), shape, f32)
        dt = jnp.exp(u * (jnp.log(0.1) - jnp.log(0.001)) + jnp.log(0.001))
        return dt + jnp.log(-jnp.expm1(-dt))

    x_prompt = nrm((BATCH, SEQ, D_MODEL), 1.0)
    x_sample = nrm((DEC_BATCH, DEC_SEQ, D_MODEL), 1.0)
    state_hgrn = nrm((DEPTH, DEC_BATCH, HG_HEADS, HG_DK, HG_DV), 0.5)
    state_gdn = nrm((DEPTH, DEC_BATCH, GDN_HEADS, GDN_DK, GDN_DV), 0.5)
    state_gdn_conv = nrm((DEPTH, DEC_BATCH, CONV_W - 1, 3 * GDN_W), 1.0)
    state_ssm = nrm((DEPTH, DEC_BATCH, M2_HEADS, M2_HEAD_DIM, M2_STATE), 0.5)
    state_ssm_conv = nrm((DEPTH, DEC_BATCH, CONV_W - 1, M2_CONV_DIM), 1.0)
    mix_norm = 1.0 + nrm((DEPTH, D_MODEL), 0.02)
    w_in = nrm((DEPTH, D_MODEL, N_IN), D_MODEL ** -0.5)
    hg_lb = nrm((DEPTH, HG_W), 1.0)
    hg_onorm = 1.0 + nrm((DEPTH, HG_DV), 0.02)
    gdn_conv = nrm((DEPTH, CONV_W, 3 * GDN_W), CONV_W ** -0.5)
    gdn_A_log = jnp.log(jax.random.uniform(next(ks), (DEPTH, GDN_HEADS), f32, 1.0, 16.0))
    gdn_dt_bias = inv_softplus_dt((DEPTH, GDN_HEADS))
    gdn_onorm = 1.0 + nrm((DEPTH, GDN_DV), 0.02)
    m2_conv_w = nrm((DEPTH, CONV_W, M2_CONV_DIM), CONV_W ** -0.5)
    m2_conv_b = nrm((DEPTH, M2_CONV_DIM), 0.02)
    m2_dt_bias = inv_softplus_dt((DEPTH, M2_HEADS))
    m2_A_log = jnp.log(jax.random.uniform(next(ks), (DEPTH, M2_HEADS), f32, 1.0, 16.0))
    m2_D = 1.0 + nrm((DEPTH, M2_HEADS), 0.1)
    m2_norm = 1.0 + nrm((DEPTH, M2_INNER), 0.02)
    w_branch = jnp.concatenate([nrm((DEPTH, HG_W, D_MODEL), HG_W ** -0.5),
                                nrm((DEPTH, GDN_W, D_MODEL), GDN_W ** -0.5),
                                nrm((DEPTH, M2_INNER, D_MODEL), M2_INNER ** -0.5)], axis=1)
    w_out = nrm((DEPTH, D_MODEL, D_MODEL), D_MODEL ** -0.5)
    ffn_norm = 1.0 + nrm((DEPTH, D_MODEL), 0.02)
    w_ffn_in = nrm((DEPTH, D_MODEL, 2 * D_FF), D_MODEL ** -0.5)
    w_ffn_out = nrm((DEPTH, D_FF, D_MODEL), D_FF ** -0.5)
    final_norm = 1.0 + nrm((D_MODEL,), 0.02)
    return {'x_prompt': x_prompt, 'x_sample': x_sample,
            'state_hgrn': state_hgrn, 'state_gdn': state_gdn, 'state_gdn_conv': state_gdn_conv,
            'state_ssm': state_ssm, 'state_ssm_conv': state_ssm_conv,
            'mix_norm': mix_norm, 'w_in': w_in, 'hg_lb': hg_lb, 'hg_onorm': hg_onorm,
            'gdn_conv': gdn_conv, 'gdn_A_log': gdn_A_log, 'gdn_dt_bias': gdn_dt_bias, 'gdn_onorm': gdn_onorm,
            'm2_conv_w': m2_conv_w, 'm2_conv_b': m2_conv_b, 'm2_dt_bias': m2_dt_bias, 'm2_A_log': m2_A_log,
            'm2_D': m2_D, 'm2_norm': m2_norm, 'w_branch': w_branch, 'w_out': w_out,
            'ffn_norm': ffn_norm, 'w_ffn_in': w_ffn_in, 'w_ffn_out': w_ffn_out, 'final_norm': final_norm}


def reference(x_prompt, x_sample, state_hgrn, state_gdn, state_gdn_conv, state_ssm, state_ssm_conv,
              mix_norm, w_in, hg_lb, hg_onorm, gdn_conv, gdn_A_log, gdn_dt_bias, gdn_onorm,
              m2_conv_w, m2_conv_b, m2_dt_bias, m2_A_log, m2_D, m2_norm, w_branch, w_out,
              ffn_norm, w_ffn_in, w_ffn_out, final_norm):
    w = {'mix_norm': mix_norm, 'w_in': w_in, 'hg_lb': hg_lb, 'hg_onorm': hg_onorm,
         'gdn_conv': gdn_conv, 'gdn_A_log': gdn_A_log, 'gdn_dt_bias': gdn_dt_bias, 'gdn_onorm': gdn_onorm,
         'm2_conv_w': m2_conv_w, 'm2_conv_b': m2_conv_b, 'm2_dt_bias': m2_dt_bias, 'm2_A_log': m2_A_log,
         'm2_D': m2_D, 'm2_norm': m2_norm, 'w_branch': w_branch, 'w_out': w_out,
         'ffn_norm': ffn_norm, 'w_ffn_in': w_ffn_in, 'w_ffn_out': w_ffn_out, 'final_norm': final_norm}
    B = x_prompt.shape[0]
    f32 = jnp.float32
    z_hg = jnp.zeros((DEPTH, B, HG_HEADS, HG_DK, HG_DV), f32)
    z_gdn = jnp.zeros((DEPTH, B, GDN_HEADS, GDN_DK, GDN_DV), f32)
    z_gc = jnp.zeros((DEPTH, B, CONV_W - 1, 3 * GDN_W), x_prompt.dtype)
    z_ssm = jnp.zeros((DEPTH, B, M2_HEADS, M2_HEAD_DIM, M2_STATE), f32)
    z_sc = jnp.zeros((DEPTH, B, CONV_W - 1, M2_CONV_DIM), x_prompt.dtype)
    y_prompt, p_hg, p_gdn, p_gc, p_ssm, p_sc = run_trunk(x_prompt, z_hg, z_gdn, z_gc, z_ssm, z_sc, w)
    y_sample, s_hg, s_gdn, s_gc, s_ssm, s_sc = run_trunk(x_sample, state_hgrn, state_gdn, state_gdn_conv,
                                                         state_ssm, state_ssm_conv, w)
    return (y_prompt, y_sample, p_hg, p_gdn, p_gc, p_ssm, p_sc, s_hg, s_gdn, s_gc, s_ssm, s_sc)
```

```python
import functools

import jax
import jax.numpy as jnp
from jax import lax
from jax.experimental import pallas as pl
from jax.experimental.pallas import tpu as pltpu

f32 = jnp.float32
bf16 = jnp.bfloat16

D_MODEL = 4096
DEPTH = 4
HEADS = 8
HDIM = 128
HG_W = HEADS * HDIM
GDN_W = HEADS * HDIM
M2_INNER = D_MODEL // 2
M2_P = 64
M2_HEADS = M2_INNER // M2_P
M2_GROUPS = 4
M2_N = 128
M2_R = M2_HEADS // M2_GROUPS
M2_GW = M2_INNER // M2_GROUPS
M2_CONV = M2_INNER + 2 * M2_GROUPS * M2_N
CONV_W = 4
D_FF = -(-8 * D_MODEL // (3 * 256)) * 256
EPS = 1e-6
CHUNK = 64
SUB = 8

C_HQ, C_HF, C_HI, C_HG = 0, 1024, 2048, 3072
C_GQKV = 4096
C_GG = 7168
C_MZ = 8192
C_MXBC = 10240
C_GATE = 13312
C_SM = 25600
N_PROJ = C_SM + 128
LANES = 128
VMEM_LIMIT = 56 * 1024 * 1024


def _cparams(n_axes):
    return pltpu.CompilerParams(dimension_semantics=("arbitrary",) * n_axes,
                                vmem_limit_bytes=VMEM_LIMIT)


def _sigmoid(x):
    return 1.0 / (1.0 + jnp.exp(-x))


def _silu(x):
    return x * _sigmoid(x)


def _softplus(x):
    return jnp.maximum(x, 0.0) + jnp.log1p(jnp.exp(-jnp.abs(x)))


def _log_sigmoid(x):
    return jnp.minimum(x, 0.0) - jnp.log1p(jnp.exp(-jnp.abs(x)))


def _dot(a, b):
    return jnp.dot(a.astype(bf16), b.astype(bf16), preferred_element_type=f32)


def _dot_nt(a, b):
    return lax.dot_general(a.astype(bf16), b.astype(bf16), (((1,), (1,)), ((), ())),
                           preferred_element_type=f32)


def _dot_tn(a, b):
    return lax.dot_general(a.astype(bf16), b.astype(bf16), (((0,), (0,)), ((), ())),
                           preferred_element_type=f32)


def _dot_hi(a, b):
    return jnp.dot(a, b, preferred_element_type=f32, precision=lax.Precision.HIGHEST)


def _split3(x):
    x1 = x.astype(bf16)
    r = x - x1.astype(f32)
    x2 = r.astype(bf16)
    x3 = (r - x2.astype(f32)).astype(bf16)
    return x1, x2, x3


def _sel_dot(sel, x):
    x1, x2, x3 = _split3(x)
    d = lambda t: jnp.dot(sel, t, preferred_element_type=f32)
    return d(x1) + d(x2) + d(x3)


def _dot_sel(x, sel):
    x1, x2, x3 = _split3(x)
    d = lambda t: jnp.dot(t, sel, preferred_element_type=f32)
    return d(x1) + d(x2) + d(x3)


def _tril_incl(n):
    r = lax.broadcasted_iota(jnp.int32, (n, n), 0)
    c = lax.broadcasted_iota(jnp.int32, (n, n), 1)
    return jnp.where(r >= c, 1.0, 0.0).astype(bf16)


def _rms(x, w):
    return x * lax.rsqrt(jnp.mean(x * x, axis=-1, keepdims=True) + EPS) * w


def _extract_col(x, lane_idx):
    lane = lax.broadcasted_iota(jnp.int32, x.shape, 1)
    return jnp.sum(jnp.where(lane == lane_idx, x, 0.0), axis=-1, keepdims=True)


def _rmsnorm_kernel(x_ref, w_ref, o_ref):
    o_ref[...] = _rms(x_ref[...], w_ref[...]).astype(o_ref.dtype)


def _rmsnorm(x, w, out_dtype, tr):
    m, d = x.shape
    return pl.pallas_call(
        _rmsnorm_kernel, grid=(m // tr,),
        in_specs=[pl.BlockSpec((tr, d), lambda i: (i, 0)), pl.BlockSpec((1, d), lambda i: (0, 0))],
        out_specs=pl.BlockSpec((tr, d), lambda i: (i, 0)),
        out_shape=jax.ShapeDtypeStruct((m, d), out_dtype),
        compiler_params=_cparams(1), name="rmsnorm")(x, w.reshape(1, d))


def _mm_plain_kernel(x_ref, w_ref, o_ref):
    o_ref[...] = jnp.dot(x_ref[...], w_ref[...], preferred_element_type=f32)


def _mm_plain(x, w, tm, tn):
    m, k = x.shape
    n = w.shape[1]
    return pl.pallas_call(
        _mm_plain_kernel, grid=(m // tm, pl.cdiv(n, tn)),
        in_specs=[pl.BlockSpec((tm, k), lambda i, j: (i, 0)), pl.BlockSpec((k, tn), lambda i, j: (0, j))],
        out_specs=pl.BlockSpec((tm, tn), lambda i, j: (i, j)),
        out_shape=jax.ShapeDtypeStruct((m, n), f32),
        compiler_params=_cparams(2), name="mm_plain")(x, w)


def _mm_resid_kernel(x_ref, w_ref, r_ref, o_ref):
    o_ref[...] = r_ref[...] + jnp.dot(x_ref[...], w_ref[...], preferred_element_type=f32)


def _mm_resid(x, w, r, tm, tn):
    m, k = x.shape
    n = w.shape[1]
    return pl.pallas_call(
        _mm_resid_kernel, grid=(m // tm, n // tn),
        in_specs=[pl.BlockSpec((tm, k), lambda i, j: (i, 0)), pl.BlockSpec((k, tn), lambda i, j: (0, j)),
                  pl.BlockSpec((tm, tn), lambda i, j: (i, j))],
        out_specs=pl.BlockSpec((tm, tn), lambda i, j: (i, j)),
        out_shape=jax.ShapeDtypeStruct((m, n), f32),
        compiler_params=_cparams(2), name="mm_resid")(x, w, r)


def _mm_swiglu_kernel(x_ref, wg_ref, wu_ref, o_ref):
    x = x_ref[...]
    g = jnp.dot(x, wg_ref[...], preferred_element_type=f32)
    u = jnp.dot(x, wu_ref[...], preferred_element_type=f32)
    o_ref[...] = (_silu(g) * u).astype(o_ref.dtype)


def _mm_swiglu(x, w, tm, tn):
    m, k = x.shape
    n = w.shape[1] // 2
    nb = n // tn
    return pl.pallas_call(
        _mm_swiglu_kernel, grid=(m // tm, nb),
        in_specs=[pl.BlockSpec((tm, k), lambda i, j: (i, 0)), pl.BlockSpec((k, tn), lambda i, j: (0, j)),
                  pl.BlockSpec((k, tn), lambda i, j: (0, j + nb))],
        out_specs=pl.BlockSpec((tm, tn), lambda i, j: (i, j)),
        out_shape=jax.ShapeDtypeStruct((m, n), bf16),
        compiler_params=_cparams(2), name="mm_swiglu")(x, w, w)


def _merge_kernel(o_ref, w_ref, ga_ref, gb_ref, gc_ref, out_ref):
    a = jnp.dot(o_ref[:, 0:HG_W], w_ref[0:HG_W, :], preferred_element_type=f32)
    b = jnp.dot(o_ref[:, HG_W:HG_W + GDN_W], w_ref[HG_W:HG_W + GDN_W, :], preferred_element_type=f32)
    c = jnp.dot(o_ref[:, HG_W + GDN_W:], w_ref[HG_W + GDN_W:, :], preferred_element_type=f32)
    out = _sigmoid(ga_ref[...]) * a + _sigmoid(gb_ref[...]) * b + _sigmoid(gc_ref[...]) * c
    out_ref[...] = out.astype(out_ref.dtype)


def _merge(o, wb, proj, tm, tn):
    m, k = o.shape
    n = wb.shape[1]
    g0 = C_GATE // tn
    gs = D_MODEL // tn
    return pl.pallas_call(
        _merge_kernel, grid=(m // tm, n // tn),
        in_specs=[pl.BlockSpec((tm, k), lambda i, j: (i, 0)), pl.BlockSpec((k, tn), lambda i, j: (0, j)),
                  pl.BlockSpec((tm, tn), lambda i, j: (i, g0 + j)),
                  pl.BlockSpec((tm, tn), lambda i, j: (i, g0 + gs + j)),
                  pl.BlockSpec((tm, tn), lambda i, j: (i, g0 + 2 * gs + j))],
        out_specs=pl.BlockSpec((tm, tn), lambda i, j: (i, j)),
        out_shape=jax.ShapeDtypeStruct((m, n), bf16),
        compiler_params=_cparams(2), name="merge")(o, wb, proj, proj, proj)


def _hgrn_inputs(qz, z, par):
    q = _silu(qz)
    a = par[0:1]
    b = par[1:2] + _log_sigmoid(z)
    m = jnp.maximum(a, b)
    logf = m + jnp.log(jnp.exp(a - m) + jnp.exp(b - m))
    k = par[2:3] / (1.0 + jnp.exp(z))
    return q, k, logf


def _hgrn_prompt_kernel(q_ref, f_ref, i_ref, g_ref, par_ref, o_ref, s_ref, st_scr, *, nchunk):
    t = pl.program_id(2)
    C = CHUNK

    @pl.when(t == 0)
    def _():
        st_scr[...] = jnp.zeros_like(st_scr)

    par = par_ref[...]
    tril = _tril_incl(C)
    lane_c = lax.broadcasted_iota(jnp.int32, (SUB, C), 1)
    sub_r = lax.broadcasted_iota(jnp.int32, (SUB, C), 0)

    def chunk(ci, carry):
        r0 = pl.multiple_of(ci * C, C)
        rows = pl.ds(r0, C)
        q, k, logf = _hgrn_inputs(q_ref[rows, :], f_ref[rows, :], par)
        v = i_ref[rows, :]
        G = _sel_dot(tril, logf)
        st = st_scr[...]
        o = _dot_nt(q * jnp.exp(G), st)
        a_rows = []
        for a in range(C // SUB):
            lo = a * SUB
            Ga, qa, ka = G[lo:lo + SUB], q[lo:lo + SUB], k[lo:lo + SUB]
            if a > 0:
                Gs = G[lo - 1:lo]
                qt = qa * jnp.exp(Ga - Gs)
                kt = k * jnp.exp(jnp.minimum(Gs - G, 0.0))
                R = jnp.where(lane_c < lo, _dot_nt(qt, kt), 0.0)
            else:
                R = jnp.zeros((SUB, C), f32)
            for jl in range(SUB):
                e = qa * ka[jl:jl + 1] * jnp.exp(jnp.minimum(Ga - Ga[jl:jl + 1], 0.0))
                c = jnp.sum(e, axis=-1, keepdims=True)
                R = jnp.where(lane_c == lo + jl, jnp.where(sub_r >= jl, c, 0.0), R)
            a_rows.append(R)
        A = jnp.concatenate(a_rows, axis=0)
        o = o + _dot(A, v)
        Gl = G[C - 1:C]
        st_scr[...] = st * jnp.exp(Gl) + _dot_tn(v, k * jnp.exp(Gl - G))
        gate = _silu(g_ref[rows, :])
        o_ref[rows, :] = (_rms(o, par[3:4]) * gate).astype(o_ref.dtype)
        return carry

    lax.fori_loop(0, nchunk, chunk, 0)

    @pl.when(t == pl.num_programs(2) - 1)
    def _():
        s_ref[0, 0] = st_scr[...].T


def _hgrn_prompt(proj, par, B, T, tb):
    nt = T // tb
    cb = lambda off: (lambda b, h, t: (b * nt + t, off // LANES + h))
    return pl.pallas_call(
        functools.partial(_hgrn_prompt_kernel, nchunk=tb // CHUNK),
        grid=(B, HEADS, nt),
        in_specs=[pl.BlockSpec((tb, LANES), cb(C_HQ)), pl.BlockSpec((tb, LANES), cb(C_HF)),
                  pl.BlockSpec((tb, LANES), cb(C_HI)), pl.BlockSpec((tb, LANES), cb(C_HG)),
                  pl.BlockSpec((8, LANES), lambda b, h, t: (0, h))],
        out_specs=[pl.BlockSpec((tb, LANES), lambda b, h, t: (b * nt + t, h)),
                   pl.BlockSpec((1, 1, HDIM, HDIM), lambda b, h, t: (b, h, 0, 0))],
        out_shape=[jax.ShapeDtypeStruct((B * T, HG_W), bf16),
                   jax.ShapeDtypeStruct((B, HEADS, HDIM, HDIM), f32)],
        scratch_shapes=[pltpu.VMEM((HDIM, HDIM), f32)],
        compiler_params=_cparams(3), name="hgrn_prompt")(proj, proj, proj, proj, par)


def _row_mask(x, j):
    r = lax.broadcasted_iota(jnp.int32, x.shape, 0)
    return jnp.where(r == j, x, 0.0)


def _hgrn_step_kernel(q_ref, f_ref, i_ref, g_ref, par_ref, s_ref, o_ref, so_ref):
    par = par_ref[...]
    q, k, logf = _hgrn_inputs(q_ref[...], f_ref[...], par)
    v = i_ref[...]
    ef = jnp.exp(logf)
    ones = jnp.ones((8, LANES), f32)
    row = lax.broadcasted_iota(jnp.int32, (8, LANES), 0)
    o = jnp.zeros((8, LANES), f32)
    for j in range(8):
        ecol = lax.dot_general(_row_mask(ef, j), ones, (((0,), (0,)), ((), ())),
                               preferred_element_type=f32, precision=lax.Precision.HIGHEST)
        s_new = s_ref[j, 0] * ecol + _dot_tn(_row_mask(k, j), v)
        so_ref[j, 0] = s_new
        o = jnp.where(row == j, _dot(q, s_new), o)
    o_ref[...] = (_rms(o, par[3:4]) * _silu(g_ref[...])).astype(o_ref.dtype)


def _hgrn_step(proj, par, state, row0, nb):
    rb = row0 // 8
    cb = lambda off: (lambda i, h: (rb + i, off // LANES + h))
    return pl.pallas_call(
        _hgrn_step_kernel, grid=(nb // 8, HEADS),
        in_specs=[pl.BlockSpec((8, LANES), cb(C_HQ)), pl.BlockSpec((8, LANES), cb(C_HF)),
                  pl.BlockSpec((8, LANES), cb(C_HI)), pl.BlockSpec((8, LANES), cb(C_HG)),
                  pl.BlockSpec((8, LANES), lambda i, h: (0, h)),
                  pl.BlockSpec((8, 1, HDIM, HDIM), lambda i, h: (i, h, 0, 0))],
        out_specs=[pl.BlockSpec((8, LANES), lambda i, h: (i, h)),
                   pl.BlockSpec((8, 1, HDIM, HDIM), lambda i, h: (i, h, 0, 0))],
        out_shape=[jax.ShapeDtypeStruct((nb, HG_W), f32),
                   jax.ShapeDtypeStruct(state.shape, f32)],
        compiler_params=_cparams(2), name="hgrn_step")(proj, proj, proj, proj, par, state)


def _l2norm(t):
    return t * lax.rsqrt(jnp.sum(t * t, axis=-1, keepdims=True) + EPS)


def _gdn_gates(sm, p2, h):
    beta = _extract_col(_sigmoid(sm), h)
    g_all = -jnp.exp(p2[1:2]) * _softplus(sm + p2[0:1])
    return beta, _extract_col(g_all, HEADS + h)


def _conv_block(xp_scr, x_ref, w, tb, first):
    @pl.when(first)
    def _():
        xp_scr[0:8, :] = jnp.zeros((8, xp_scr.shape[1]), f32)

    xp_scr[8:8 + tb, :] = x_ref[...]
    out = xp_scr[5:5 + tb, :] * w[0:1]
    for j in range(1, CONV_W):
        out = out + xp_scr[5 + j:5 + j + tb, :] * w[j:j + 1]
    xp_scr[5:8, :] = xp_scr[tb + 5:tb + 8, :]
    return out


def _gdn_prompt_kernel(q_ref, k_ref, v_ref, g_ref, sm_ref, wq_ref, wk_ref, wv_ref, p2_ref,
                       o_ref, s_ref, xq, xk, xv, cq, ck, cv, s_scr, *, nchunk, tb):
    h = pl.program_id(1)
    t = pl.program_id(2)
    C = CHUNK
    first = t == 0

    @pl.when(first)
    def _():
        s_scr[...] = jnp.zeros_like(s_scr)

    cq[...] = _silu(_conv_block(xq, q_ref, wq_ref[...], tb, first))
    ck[...] = _silu(_conv_block(xk, k_ref, wk_ref[...], tb, first))
    cv[...] = _silu(_conv_block(xv, v_ref, wv_ref[...], tb, first))

    p2 = p2_ref[...]
    tril = _tril_incl(C)
    r = lax.broadcasted_iota(jnp.int32, (C, C), 0)
    c = lax.broadcasted_iota(jnp.int32, (C, C), 1)
    ustrict = jnp.where(r > c, 1.0, 0.0)
    eye = jnp.where(r == c, 1.0, 0.0)

    def chunk(ci, carry):
        r0 = pl.multiple_of(ci * C, C)
        rows = pl.ds(r0, C)
        q = _l2norm(cq[rows, :]) * (HDIM ** -0.5)
        k = _l2norm(ck[rows, :])
        v = cv[rows, :]
        beta, g = _gdn_gates(sm_ref[rows, :], p2, h)
        gb = jnp.broadcast_to(g, (C, LANES))
        Gb = _sel_dot(tril, gb)
        Dm = _sel_dot(tril, gb[:, 0:C] * ustrict)
        dec = jnp.where(r >= c, jnp.exp(jnp.minimum(Dm, 0.0)), 0.0)
        eG = jnp.exp(Gb)
        kb = k * beta
        N = jnp.where(r > c, -_dot_nt(kb, k) * dec, 0.0)
        Tinv = eye + N
        P = N
        for _ in range(5):
            P = _dot_hi(P, P)
            Tinv = Tinv + _dot_hi(Tinv, P)
        S = s_scr[...]
        U = _dot_hi(Tinv, v * beta)
        Wk = _dot_hi(Tinv, kb * eG)
        Vn = U - _dot(Wk, S)
        Aqk = _dot_nt(q, k) * dec
        o = _dot(q * eG, S) + _dot(Aqk, Vn)
        Gl = Gb[C - 1:C]
        s_scr[...] = S * jnp.exp(Gl) + _dot_tn(k * jnp.exp(Gl - Gb), Vn)
        gate = _silu(g_ref[rows, :])
        o_ref[rows, :] = (_rms(o, p2[2:3]) * gate).astype(o_ref.dtype)
        return carry

    lax.fori_loop(0, nchunk, chunk, 0)

    @pl.when(t == pl.num_programs(2) - 1)
    def _():
        s_ref[0, 0] = s_scr[...]


def _gdn_prompt(proj, pconv, p2, B, T, tb):
    nt = T // tb
    cb = lambda off: (lambda b, h, t: (b * nt + t, off // LANES + h))
    wb = lambda off: (lambda b, h, t: (0, off // LANES + h))
    return pl.pallas_call(
        functools.partial(_gdn_prompt_kernel, nchunk=tb // CHUNK, tb=tb),
        grid=(B, HEADS, nt),
        in_specs=[pl.BlockSpec((tb, LANES), cb(C_GQKV)), pl.BlockSpec((tb, LANES), cb(C_GQKV + GDN_W)),
                  pl.BlockSpec((tb, LANES), cb(C_GQKV + 2 * GDN_W)), pl.BlockSpec((tb, LANES), cb(C_GG)),
                  pl.BlockSpec((tb, LANES), lambda b, h, t: (b * nt + t, C_SM // LANES)),
                  pl.BlockSpec((8, LANES), wb(0)), pl.BlockSpec((8, LANES), wb(GDN_W)),
                  pl.BlockSpec((8, LANES), wb(2 * GDN_W)),
                  pl.BlockSpec((8, LANES), lambda b, h, t: (0, 0))],
        out_specs=[pl.BlockSpec((tb, LANES), lambda b, h, t: (b * nt + t, h)),
                   pl.BlockSpec((1, 1, HDIM, HDIM), lambda b, h, t: (b, h, 0, 0))],
        out_shape=[jax.ShapeDtypeStruct((B * T, GDN_W), bf16),
                   jax.ShapeDtypeStruct((B, HEADS, HDIM, HDIM), f32)],
        scratch_shapes=[pltpu.VMEM((tb + 8, LANES), f32)] * 3 + [pltpu.VMEM((tb, LANES), f32)] * 3
        + [pltpu.VMEM((HDIM, HDIM), f32)],
        compiler_params=_cparams(3), name="gdn_prompt")(proj, proj, proj, proj, proj, pconv, pconv, pconv, p2)


def _conv_step(x, b0, b1, b2, w):
    return b0 * w[0:1] + b1 * w[1:2] + b2 * w[2:3] + x * w[3:4]


def _gdn_step_kernel(q_ref, k_ref, v_ref, g_ref, sm_ref, q0, q1, q2, k0, k1, k2, v0, v1, v2,
                     wq_ref, wk_ref, wv_ref, p2_ref, s_ref, o_ref, so_ref):
    h = pl.program_id(1)
    p2 = p2_ref[...]
    q = _l2norm(_silu(_conv_step(q_ref[...], q0[...], q1[...], q2[...], wq_ref[...]))) * (HDIM ** -0.5)
    k = _l2norm(_silu(_conv_step(k_ref[...], k0[...], k1[...], k2[...], wk_ref[...])))
    v = _silu(_conv_step(v_ref[...], v0[...], v1[...], v2[...], wv_ref[...]))
    beta, g = _gdn_gates(sm_ref[...], p2, h)
    eg = jnp.exp(jnp.broadcast_to(g, (8, LANES)))
    row = lax.broadcasted_iota(jnp.int32, (8, LANES), 0)
    qs = jnp.zeros((8, LANES), f32)
    ks = jnp.zeros((8, LANES), f32)
    for j in range(8):
        s = s_ref[j, 0]
        qs = jnp.where(row == j, _dot(q, s), qs)
        ks = jnp.where(row == j, _dot(k, s), ks)
    vn = beta * v - (beta * eg) * ks
    o = eg * qs + jnp.sum(q * k, axis=-1, keepdims=True) * vn
    for j in range(8):
        so_ref[j, 0] = s_ref[j, 0] * jnp.broadcast_to(eg[j:j + 1, 0:1], (HDIM, HDIM)) + _dot_tn(_row_mask(k, j), vn)
    o_ref[...] = (_rms(o, p2[2:3]) * _silu(g_ref[...])).astype(o_ref.dtype)


def _gdn_step(proj, conv_state, pconv, p2, state, row0, nb):
    rb = row0 // 8
    cb = lambda off: (lambda i, h: (rb + i, off // LANES + h))
    wb = lambda off: (lambda i, h: (0, off // LANES + h))
    tap = lambda off, j: pl.BlockSpec((None, 8, LANES), lambda i, h: (j, i, off // LANES + h))
    taps = [tap(off, j) for off in (0, GDN_W, 2 * GDN_W) for j in range(CONV_W - 1)]
    conv_state = jnp.swapaxes(conv_state, 0, 1)
    return pl.pallas_call(
        _gdn_step_kernel, grid=(nb // 8, HEADS),
        in_specs=[pl.BlockSpec((8, LANES), cb(C_GQKV)), pl.BlockSpec((8, LANES), cb(C_GQKV + GDN_W)),
                  pl.BlockSpec((8, LANES), cb(C_GQKV + 2 * GDN_W)), pl.BlockSpec((8, LANES), cb(C_GG)),
                  pl.BlockSpec((8, LANES), lambda i, h: (rb + i, C_SM // LANES))] + taps
        + [pl.BlockSpec((8, LANES), wb(0)), pl.BlockSpec((8, LANES), wb(GDN_W)),
           pl.BlockSpec((8, LANES), wb(2 * GDN_W)), pl.BlockSpec((8, LANES), lambda i, h: (0, 0)),
           pl.BlockSpec((8, 1, HDIM, HDIM), lambda i, h: (i, h, 0, 0))],
        out_specs=[pl.BlockSpec((8, LANES), lambda i, h: (i, h)),
                   pl.BlockSpec((8, 1, HDIM, HDIM), lambda i, h: (i, h, 0, 0))],
        out_shape=[jax.ShapeDtypeStruct((nb, GDN_W), f32), jax.ShapeDtypeStruct(state.shape, f32)],
        compiler_params=_cparams(2), name="gdn_step")(
            proj, proj, proj, proj, proj, *([conv_state] * 9), pconv, pconv, pconv, p2, state)


def _head_select(g, width):
    lane = lax.broadcasted_iota(jnp.int32, (LANES, M2_R * width), 0)
    col = lax.broadcasted_iota(jnp.int32, (LANES, M2_R * width), 1)
    return jnp.where(lane == 16 + g * M2_R + col // width, 1.0, 0.0).astype(bf16)


def _ssd_prompt_kernel(z_ref, x_ref, b_ref, c_ref, sm_ref, px_ref, pb_ref, pc_ref,
                       o_ref, s_ref, xx, xb, xc, cx, cb_s, cc, y_scr, st_scr, *, nchunk, tb):
    g = pl.program_id(1)
    t = pl.program_id(2)
    C = CHUNK
    P = M2_P
    first = t == 0

    @pl.when(first)
    def _():
        st_scr[...] = jnp.zeros_like(st_scr)

    px = px_ref[...]
    pb = pb_ref[...]
    pc = pc_ref[...]
    cx[...] = _silu(_conv_block(xx, x_ref, px, tb, first) + px[4:5])
    cb_s[...] = _silu(_conv_block(xb, b_ref, pb, tb, first) + pb[4:5])
    cc[...] = _silu(_conv_block(xc, c_ref, pc, tb, first) + pc[4:5])

    tril = _tril_incl(C)
    sel = _head_select(g, P)
    r = lax.broadcasted_iota(jnp.int32, (C, M2_GW), 0)
    cmod = lax.broadcasted_iota(jnp.int32, (C, M2_GW), 1) % P
    ustrict = jnp.where(r > cmod, 1.0, 0.0)
    causal = r >= cmod
    neg_a = -jnp.exp(px[6:7])

    def chunk(ci, carry):
        r0 = pl.multiple_of(ci * C, C)
        rows = pl.ds(r0, C)
        x = cx[rows, :]
        Bm = cb_s[rows, :]
        Cm = cc[rows, :]
        dt = _softplus(_dot_sel(sm_ref[rows, :], sel) + px[5:6])
        a = dt * neg_a
        GW = _sel_dot(tril, a)
        dec = jnp.where(causal, jnp.exp(jnp.minimum(_sel_dot(tril, a * ustrict), 0.0)), 0.0)
        CB = _dot_nt(Cm, Bm)
        xdt = x * dt
        st = st_scr[...]
        y_inter = _dot(Cm, st) * jnp.exp(GW)
        for h in range(M2_R):
            sl = slice(h * P, (h + 1) * P)
            y_scr[:, sl] = _dot(CB * dec[:, sl], xdt[:, sl])
        y = y_scr[...] + y_inter + x * px[7:8]
        Gl = GW[C - 1:C]
        st_scr[...] = st * jnp.exp(Gl) + _dot_tn(Bm, xdt * jnp.exp(Gl - GW))
        y = y * _silu(z_ref[rows, :])
        o_ref[rows, :] = _rms(y, px[8:9]).astype(o_ref.dtype)
        return carry

    lax.fori_loop(0, nchunk, chunk, 0)

    @pl.when(t == pl.num_programs(2) - 1)
    def _():
        s_ref[0] = st_scr[...].T.reshape(M2_R, P, M2_N)


def _ssd_prompt(proj, px, pbc, B, T, tb):
    nt = T // tb
    xs = lambda off: (lambda b, g, t: (b * nt + t, off // M2_GW + g))
    bc = lambda off: (lambda b, g, t: (b * nt + t, off // LANES + g))
    return pl.pallas_call(
        functools.partial(_ssd_prompt_kernel, nchunk=tb // CHUNK, tb=tb),
        grid=(B, M2_GROUPS, nt),
        in_specs=[pl.BlockSpec((tb, M2_GW), xs(C_MZ)), pl.BlockSpec((tb, M2_GW), xs(C_MXBC)),
                  pl.BlockSpec((tb, LANES), bc(C_MXBC + M2_INNER)),
                  pl.BlockSpec((tb, LANES), bc(C_MXBC + M2_INNER + M2_GROUPS * M2_N)),
                  pl.BlockSpec((tb, LANES), lambda b, g, t: (b * nt + t, C_SM // LANES)),
                  pl.BlockSpec((16, M2_GW), lambda b, g, t: (0, g)),
                  pl.BlockSpec((8, LANES), lambda b, g, t: (0, g)),
                  pl.BlockSpec((8, LANES), lambda b, g, t: (0, M2_GROUPS + g))],
        out_specs=[pl.BlockSpec((tb, M2_GW), lambda b, g, t: (b * nt + t, g)),
                   pl.BlockSpec((1, M2_R, M2_P, M2_N), lambda b, g, t: (b, g, 0, 0))],
        out_shape=[jax.ShapeDtypeStruct((B * T, M2_INNER), bf16),
                   jax.ShapeDtypeStruct((B, M2_HEADS, M2_P, M2_N), f32)],
        scratch_shapes=[pltpu.VMEM((tb + 8, M2_GW), f32), pltpu.VMEM((tb + 8, LANES), f32),
                        pltpu.VMEM((tb + 8, LANES), f32), pltpu.VMEM((tb, M2_GW), f32),
                        pltpu.VMEM((tb, LANES), f32), pltpu.VMEM((tb, LANES), f32),
                        pltpu.VMEM((CHUNK, M2_GW), f32), pltpu.VMEM((M2_N, M2_GW), f32)],
        compiler_params=_cparams(3), name="ssd_prompt")(proj, proj, proj, proj, proj, px, pbc, pbc)


def _ssd_step_kernel(z_ref, x_ref, b_ref, c_ref, sm_ref, x0, x1, x2, b0, b1, b2, c0, c1, c2,
                     px_ref, pb_ref, pc_ref, s_ref, o_ref, so_ref, y_scr):
    g = pl.program_id(1)
    P = M2_P
    px = px_ref[...]
    pb = pb_ref[...]
    pc = pc_ref[...]
    x = _silu(_conv_step(x_ref[...], x0[...], x1[...], x2[...], px) + px[4:5])
    Bm = _silu(_conv_step(b_ref[...], b0[...], b1[...], b2[...], pb) + pb[4:5])
    Cm = _silu(_conv_step(c_ref[...], c0[...], c1[...], c2[...], pc) + pc[4:5])
    sm = sm_ref[...]
    dt = _softplus(_dot_sel(sm, _head_select(g, P)) + px[5:6])
    ea = jnp.exp(dt * -jnp.exp(px[6:7]))
    xdt = x * dt
    row = lax.broadcasted_iota(jnp.int32, (8, P), 0)
    for h in range(M2_R):
        sl = slice(h * P, (h + 1) * P)
        yh = jnp.zeros((8, P), f32)
        for j in range(8):
            s = s_ref[j, h]
            yh = jnp.where(row == j, _dot_nt(Cm, s), yh)
            scale = jnp.broadcast_to(ea[j:j + 1, h * P:h * P + 1], (P, M2_N))
            so_ref[j, h] = s * scale + _dot_tn(_row_mask(xdt[:, sl], j), Bm)
        y_scr[:, sl] = yh
    y = y_scr[...] * ea + jnp.sum(Cm * Bm, axis=-1, keepdims=True) * xdt + x * px[7:8]
    y = y * _silu(z_ref[...])
    o_ref[...] = _rms(y, px[8:9]).astype(o_ref.dtype)


def _ssd_step(proj, conv_state, px, pbc, state, row0, nb):
    rb = row0 // 8
    xs = lambda off: (lambda i, g: (rb + i, off // M2_GW + g))
    bc = lambda off: (lambda i, g: (rb + i, off // LANES + g))
    xtap = lambda j: pl.BlockSpec((None, 8, M2_GW), lambda i, g: (j, i, g))
    btap = lambda off, j: pl.BlockSpec((None, 8, LANES), lambda i, g: (j, i, off // LANES + g))
    conv_state = jnp.swapaxes(conv_state, 0, 1)
    taps = ([xtap(j) for j in range(3)] + [btap(M2_INNER, j) for j in range(3)]
            + [btap(M2_INNER + M2_GROUPS * M2_N, j) for j in range(3)])
    return pl.pallas_call(
        _ssd_step_kernel, grid=(nb // 8, M2_GROUPS),
        in_specs=[pl.BlockSpec((8, M2_GW), xs(C_MZ)), pl.BlockSpec((8, M2_GW), xs(C_MXBC)),
                  pl.BlockSpec((8, LANES), bc(C_MXBC + M2_INNER)),
                  pl.BlockSpec((8, LANES), bc(C_MXBC + M2_INNER + M2_GROUPS * M2_N)),
                  pl.BlockSpec((8, LANES), lambda i, g: (rb + i, C_SM // LANES))] + taps
        + [pl.BlockSpec((16, M2_GW), lambda i, g: (0, g)),
           pl.BlockSpec((8, LANES), lambda i, g: (0, g)),
           pl.BlockSpec((8, LANES), lambda i, g: (0, M2_GROUPS + g)),
           pl.BlockSpec((8, M2_R, M2_P, M2_N), lambda i, g: (i, g, 0, 0))],
        out_specs=[pl.BlockSpec((8, M2_GW), lambda i, g: (i, g)),
                   pl.BlockSpec((8, M2_R, M2_P, M2_N), lambda i, g: (i, g, 0, 0))],
        out_shape=[jax.ShapeDtypeStruct((nb, M2_INNER), f32), jax.ShapeDtypeStruct(state.shape, f32)],
        scratch_shapes=[pltpu.VMEM((8, M2_GW), f32)],
        compiler_params=_cparams(2), name="ssd_step")(
            proj, proj, proj, proj, proj, *([conv_state] * 9), px, pbc, pbc, state)


def _pad_rows(a, rows):
    return jnp.pad(a, ((0, 0), (0, rows - a.shape[1]), (0, 0)))


def _pack_params(hg_lb, hg_onorm, gdn_conv, gdn_A_log, gdn_dt_bias, gdn_onorm,
                 m2_conv_w, m2_conv_b, m2_dt_bias, m2_A_log, m2_D, m2_norm):
    lb = jnp.cumsum(jax.nn.softmax(hg_lb.astype(f32), axis=0), axis=0)
    lb = lb - lb[0]
    hg_par = _pad_rows(jnp.stack([jnp.log(lb), jnp.log1p(-lb), 1.0 - lb,
                                  jnp.tile(hg_onorm, (1, HEADS))], axis=1), 8)
    gdn_pconv = _pad_rows(gdn_conv, 8)
    lane_pad = lambda a: jnp.pad(a, ((0, 0), (HEADS, LANES - 2 * HEADS)))
    gdn_p2 = _pad_rows(jnp.stack([lane_pad(gdn_dt_bias), lane_pad(gdn_A_log), gdn_onorm], axis=1), 8)
    rep = lambda a: jnp.repeat(a, M2_P, axis=1)
    ssd_px = _pad_rows(jnp.concatenate(
        [m2_conv_w[:, :, :M2_INNER],
         jnp.stack([m2_conv_b[:, :M2_INNER], rep(m2_dt_bias), rep(m2_A_log), rep(m2_D), m2_norm], axis=1)],
        axis=1), 16)
    ssd_pbc = _pad_rows(jnp.concatenate([m2_conv_w[:, :, M2_INNER:], m2_conv_b[:, None, M2_INNER:]], axis=1), 8)
    return hg_par, gdn_pconv, gdn_p2, ssd_px, ssd_pbc


def _reorder_w_in(w_in):
    s = [0, 7168, 7176, 7184, 8208, 10256, 13328, 13360, 25648]
    main = [w_in[..., s[0]:s[1]], w_in[..., s[3]:s[4]], w_in[..., s[4]:s[5]], w_in[..., s[5]:s[6]],
            w_in[..., s[7]:s[8]]]
    small = [w_in[..., s[1]:s[3]], w_in[..., s[6]:s[7]]]
    pad = jnp.zeros(w_in.shape[:-1] + (LANES - 48,), w_in.dtype)
    return jnp.concatenate(main + small + [pad], axis=-1).astype(bf16)


def _trunk(x_all, n_prompt, B, T, states, w, *, tm, tb):
    st_hg, st_gdn, st_gc, st_ssm, st_sc = states
    nb = x_all.shape[0] - n_prompt
    hg_par, gdn_pconv, gdn_p2, ssd_px, ssd_pbc = w["packed"]
    outs = {k: [] for k in ("p_hg", "p_gdn", "p_gc", "p_ssm", "p_sc", "s_hg", "s_gdn", "s_gc", "s_ssm", "s_sc")}
    x = x_all
    for l in range(DEPTH):
        h = _rmsnorm(x, w["mix_norm"][l], bf16, tm // 4)
        proj = _mm_plain(h, w["w_in"][l], tm, 512)
        oa_p, s1p = _hgrn_prompt(proj, hg_par[l], B, T, tb)
        ob_p, s2p = _gdn_prompt(proj, gdn_pconv[l], gdn_p2[l], B, T, tb)
        oc_p, s3p = _ssd_prompt(proj, ssd_px[l], ssd_pbc[l], B, T, tb // 2)
        oa_s, s1s = _hgrn_step(proj, hg_par[l], st_hg[l], n_prompt, nb)
        ob_s, s2s = _gdn_step(proj, st_gc[l], gdn_pconv[l], gdn_p2[l], st_gdn[l], n_prompt, nb)
        oc_s, s3s = _ssd_step(proj, st_sc[l], ssd_px[l], ssd_pbc[l], st_ssm[l], n_prompt, nb)
        o = jnp.concatenate([jnp.concatenate([oa_p, ob_p, oc_p], axis=1),
                             jnp.concatenate([oa_s, ob_s, oc_s], axis=1).astype(bf16)], axis=0)
        merged = _merge(o, w["w_branch"][l], proj, tm, 512)
        x = _mm_resid(merged, w["w_out"][l], x, tm, 512)
        h2 = _rmsnorm(x, w["ffn_norm"][l], bf16, tm // 4)
        act = _mm_swiglu(h2, w["w_ffn_in"][l], tm, 256)
        x = _mm_resid(act, w["w_ffn_out"][l], x, tm // 2, 256)
        pp = proj[:n_prompt].reshape(B, T, N_PROJ)[:, T - (CONV_W - 1):]
        ps = proj[n_prompt:, None, :]
        outs["p_hg"].append(s1p)
        outs["p_gdn"].append(s2p)
        outs["p_gc"].append(pp[..., C_GQKV:C_GQKV + 3 * GDN_W])
        outs["p_ssm"].append(s3p)
        outs["p_sc"].append(pp[..., C_MXBC:C_MXBC + M2_CONV])
        outs["s_hg"].append(s1s)
        outs["s_gdn"].append(s2s)
        outs["s_gc"].append(jnp.concatenate([st_gc[l][:, 1:], ps[..., C_GQKV:C_GQKV + 3 * GDN_W]], axis=1))
        outs["s_ssm"].append(s3s)
        outs["s_sc"].append(jnp.concatenate([st_sc[l][:, 1:], ps[..., C_MXBC:C_MXBC + M2_CONV]], axis=1))
    y = _rmsnorm(x, w["final_norm"], f32, tm // 4)
    return y, {k: jnp.stack(v) for k, v in outs.items()}


def kernel(x_prompt, x_sample, state_hgrn, state_gdn, state_gdn_conv, state_ssm, state_ssm_conv, mix_norm, w_in, hg_lb, hg_onorm, gdn_conv, gdn_A_log, gdn_dt_bias, gdn_onorm, m2_conv_w, m2_conv_b, m2_dt_bias, m2_A_log, m2_D, m2_norm, w_branch, w_out, ffn_norm, w_ffn_in, w_ffn_out, final_norm):
    B, T, D = x_prompt.shape
    nb = x_sample.shape[0]
    w = {"mix_norm": mix_norm, "ffn_norm": ffn_norm, "final_norm": final_norm,
         "w_in": _reorder_w_in(w_in), "w_branch": w_branch.astype(bf16), "w_out": w_out.astype(bf16),
         "w_ffn_in": w_ffn_in.astype(bf16), "w_ffn_out": w_ffn_out.astype(bf16),
         "packed": _pack_params(hg_lb, hg_onorm, gdn_conv, gdn_A_log, gdn_dt_bias, gdn_onorm,
                                m2_conv_w, m2_conv_b, m2_dt_bias, m2_A_log, m2_D, m2_norm)}
    x_all = jnp.concatenate([x_prompt.reshape(B * T, D), x_sample.reshape(nb, D)], axis=0)
    states = (state_hgrn, state_gdn, state_gdn_conv, state_ssm, state_ssm_conv)
    y, o = _trunk(x_all, B * T, B, T, states, w, tm=832, tb=512)
    return (y[:B * T].reshape(B, T, D), y[B * T:].reshape(nb, 1, D),
            o["p_hg"], o["p_gdn"], o["p_gc"], o["p_ssm"], o["p_sc"],
            o["s_hg"], o["s_gdn"], o["s_gc"], o["s_ssm"], o["s_sc"])
```

```python
import functools

import jax
import jax.numpy as jnp
from jax import lax
from jax.experimental import pallas as pl
from jax.experimental.pallas import tpu as pltpu

f32 = jnp.float32
bf16 = jnp.bfloat16

D_MODEL = 4096
DEPTH = 4
HEADS = 8
HDIM = 128
HG_W = HEADS * HDIM
GDN_W = HEADS * HDIM
M2_INNER = D_MODEL // 2
M2_P = 64
M2_HEADS = M2_INNER // M2_P
M2_GROUPS = 4
M2_N = 128
M2_R = M2_HEADS // M2_GROUPS
M2_GW = M2_INNER // M2_GROUPS
M2_BC = M2_GROUPS * M2_N
M2_CONV = M2_INNER + 2 * M2_BC
CONV_W = 4
EPS = 1e-6
CHUNK = 64
SUB = 8
TRI = 16
LANES = 128
STEP_ROWS = 16
VMEM_LIMIT = 56 * 1024 * 1024

A_HQ, A_HF, A_HI, A_HG, A_GQKV = 0, 1024, 2048, 3072, 4096
B_GG, B_MZ, B_MXBC = 0, 1024, 3072
S_BETA, S_A, S_DT = 0, 8, 16
O_HG, O_GDN, O_M2 = 0, HG_W, HG_W + GDN_W


def _cparams(n_axes):
    return pltpu.CompilerParams(dimension_semantics=("arbitrary",) * n_axes,
                                vmem_limit_bytes=VMEM_LIMIT)


def _sigmoid(x):
    return 1.0 / (1.0 + jnp.exp(-x))


def _silu(x):
    return x * _sigmoid(x)


def _softplus(x):
    return jnp.maximum(x, 0.0) + jnp.log1p(jnp.exp(-jnp.abs(x)))


def _log_sigmoid(x):
    return jnp.minimum(x, 0.0) - jnp.log1p(jnp.exp(-jnp.abs(x)))


def _mxu(a, b, dims):
    return lax.dot_general(a, b, (dims, ((), ())), preferred_element_type=f32)


def _dot(a, b):
    return _mxu(a.astype(bf16), b.astype(bf16), ((1,), (0,)))


def _dot_nt(a, b):
    return _mxu(a.astype(bf16), b.astype(bf16), ((1,), (1,)))


def _dot_tn(a, b):
    return _mxu(a.astype(bf16), b.astype(bf16), ((0,), (0,)))


def _split2(x):
    x1 = x.astype(bf16)
    return x1, (x - x1.astype(f32)).astype(bf16)


def _split3(x):
    x1 = x.astype(bf16)
    r = x - x1.astype(f32)
    x2 = r.astype(bf16)
    return x1, x2, (r - x2.astype(f32)).astype(bf16)


def _dot2(a, b):
    a1, a2 = _split2(a)
    b1, b2 = _split2(b)
    d = lambda x, y: _mxu(x, y, ((1,), (0,)))
    return d(a1, b1) + (d(a1, b2) + d(a2, b1))


def _sel_dot(sel, x):
    return sum(_mxu(sel, t, ((1,), (0,))) for t in _split3(x))


def _dot_sel(x, sel):
    return sum(_mxu(t, sel, ((1,), (0,))) for t in _split3(x))


def _bcast_cols_tn(x_masked, ones):
    return sum(_mxu(t, ones, ((0,), (0,))) for t in _split3(x_masked))


def _tril_incl(n):
    r = lax.broadcasted_iota(jnp.int32, (n, n), 0)
    c = lax.broadcasted_iota(jnp.int32, (n, n), 1)
    return jnp.where(r >= c, 1.0, 0.0).astype(bf16)


def _rms(x, w):
    return x * lax.rsqrt(jnp.mean(x * x, axis=-1, keepdims=True) + EPS) * w


def _extract_col(x, lane_idx):
    lane = lax.broadcasted_iota(jnp.int32, x.shape, 1)
    return jnp.sum(jnp.where(lane == lane_idx, x, 0.0), axis=-1, keepdims=True)


def _row_mask(x, j):
    r = lax.broadcasted_iota(jnp.int32, x.shape, 0)
    return jnp.where(r == j, x, 0.0)


def _layer_spec(block, l, index_fn):
    return pl.BlockSpec((None,) + block, lambda *g: (l,) + index_fn(*g))


def _any_spec():
    return pl.BlockSpec(memory_space=pl.ANY)


def _rmsnorm_kernel(x_ref, w_ref, o_ref):
    o_ref[...] = _rms(x_ref[...], w_ref[...]).astype(o_ref.dtype)


def _rmsnorm(x, w, out_dtype, tr):
    m, d = x.shape
    return pl.pallas_call(
        _rmsnorm_kernel, grid=(m // tr,),
        in_specs=[pl.BlockSpec((tr, d), lambda i: (i, 0)), pl.BlockSpec((1, d), lambda i: (0, 0))],
        out_specs=pl.BlockSpec((tr, d), lambda i: (i, 0)),
        out_shape=jax.ShapeDtypeStruct((m, d), out_dtype),
        compiler_params=_cparams(1), name="rmsnorm")(x, w.reshape(1, d))


def _mm_plain_kernel(x_ref, w_ref, o_ref):
    o_ref[...] = jnp.dot(x_ref[...], w_ref[...], preferred_element_type=f32)


def _mm_plain(x, w, l, tm, tn):
    m, k = x.shape
    n = w.shape[2]
    return pl.pallas_call(
        _mm_plain_kernel, grid=(m // tm, n // tn),
        in_specs=[pl.BlockSpec((tm, k), lambda i, j: (i, 0)), _layer_spec((k, tn), l, lambda i, j: (0, j))],
        out_specs=pl.BlockSpec((tm, tn), lambda i, j: (i, j)),
        out_shape=jax.ShapeDtypeStruct((m, n), f32),
        compiler_params=_cparams(2), name="mm_plain")(x, w)


def _mm_resid_kernel(x_ref, w_ref, r_ref, o_ref):
    o_ref[...] = r_ref[...] + jnp.dot(x_ref[...], w_ref[...], preferred_element_type=f32)


def _mm_resid(x, w, l, r, tm, tn):
    m, k = x.shape
    n = w.shape[2]
    return pl.pallas_call(
        _mm_resid_kernel, grid=(m // tm, n // tn),
        in_specs=[pl.BlockSpec((tm, k), lambda i, j: (i, 0)), _layer_spec((k, tn), l, lambda i, j: (0, j)),
                  pl.BlockSpec((tm, tn), lambda i, j: (i, j))],
        out_specs=pl.BlockSpec((tm, tn), lambda i, j: (i, j)),
        out_shape=jax.ShapeDtypeStruct((m, n), f32),
        compiler_params=_cparams(2), name="mm_resid")(x, w, r)


def _mm_swiglu_kernel(x_ref, wg_ref, wu_ref, o_ref):
    x = x_ref[...]
    g = jnp.dot(x, wg_ref[...], preferred_element_type=f32)
    u = jnp.dot(x, wu_ref[...], preferred_element_type=f32)
    o_ref[...] = (_silu(g) * u).astype(o_ref.dtype)


def _mm_swiglu(x, w, l, tm, tn):
    m, k = x.shape
    n = w.shape[2] // 2
    nb = n // tn
    return pl.pallas_call(
        _mm_swiglu_kernel, grid=(m // tm, nb),
        in_specs=[pl.BlockSpec((tm, k), lambda i, j: (i, 0)), _layer_spec((k, tn), l, lambda i, j: (0, j)),
                  _layer_spec((k, tn), l, lambda i, j: (0, j + nb))],
        out_specs=pl.BlockSpec((tm, tn), lambda i, j: (i, j)),
        out_shape=jax.ShapeDtypeStruct((m, n), bf16),
        compiler_params=_cparams(2), name="mm_swiglu")(x, w, w)


def _merge_kernel(o_ref, w_ref, ga_ref, gb_ref, gc_ref, out_ref):
    a = jnp.dot(o_ref[:, O_HG:O_GDN], w_ref[O_HG:O_GDN, :], preferred_element_type=f32)
    b = jnp.dot(o_ref[:, O_GDN:O_M2], w_ref[O_GDN:O_M2, :], preferred_element_type=f32)
    c = jnp.dot(o_ref[:, O_M2:], w_ref[O_M2:, :], preferred_element_type=f32)
    out = _sigmoid(ga_ref[...]) * a + _sigmoid(gb_ref[...]) * b + _sigmoid(gc_ref[...]) * c
    out_ref[...] = out.astype(out_ref.dtype)


def _merge(o, wb, l, gates, tm, tn):
    m, k = o.shape
    n = wb.shape[2]
    gs = D_MODEL // tn
    return pl.pallas_call(
        _merge_kernel, grid=(m // tm, n // tn),
        in_specs=[pl.BlockSpec((tm, k), lambda i, j: (i, 0)), _layer_spec((k, tn), l, lambda i, j: (0, j)),
                  pl.BlockSpec((tm, tn), lambda i, j: (i, j)),
                  pl.BlockSpec((tm, tn), lambda i, j: (i, gs + j)),
                  pl.BlockSpec((tm, tn), lambda i, j: (i, 2 * gs + j))],
        out_specs=pl.BlockSpec((tm, tn), lambda i, j: (i, j)),
        out_shape=jax.ShapeDtypeStruct((m, n), bf16),
        compiler_params=_cparams(2), name="merge")(o, wb, gates, gates, gates)


def _alias_last_input(n_in, has_buf):
    return {n_in - 1: 0} if has_buf else {}


def _hgrn_inputs(qz, z, par):
    q = _silu(qz)
    a = par[0:1]
    b = par[1:2] + _log_sigmoid(z)
    m = jnp.maximum(a, b)
    logf = m + jnp.log(jnp.exp(a - m) + jnp.exp(b - m))
    k = par[2:3] / (1.0 + jnp.exp(z))
    return q, k, logf


def _hgrn_chunk(q, k, logf, v, st, tril, lane_c, sub_r):
    C = CHUNK
    G = _sel_dot(tril, logf)
    o = _dot_nt(q * jnp.exp(G), st)
    a_rows = []
    for a in range(C // SUB):
        lo = a * SUB
        Ga, qa, ka = G[lo:lo + SUB], q[lo:lo + SUB], k[lo:lo + SUB]
        if a > 0:
            Gs = G[lo - 1:lo]
            qt = qa * jnp.exp(Ga - Gs)
            kt = k * jnp.exp(jnp.minimum(Gs - G, 0.0))
            R = jnp.where(lane_c < lo, _dot_nt(qt, kt), 0.0)
        else:
            R = jnp.zeros((SUB, C), f32)
        for jl in range(SUB):
            e = qa * ka[jl:jl + 1] * jnp.exp(jnp.minimum(Ga - Ga[jl:jl + 1], 0.0))
            c = jnp.sum(e, axis=-1, keepdims=True)
            R = jnp.where(lane_c == lo + jl, jnp.where(sub_r >= jl, c, 0.0), R)
        a_rows.append(R)
    A = jnp.concatenate(a_rows, axis=0)
    o = o + _dot(A, v)
    Gl = G[C - 1:C]
    return o, st * jnp.exp(Gl) + _dot_tn(v, k * jnp.exp(Gl - G))


def _hgrn_prompt_kernel(q_ref, f_ref, i_ref, g_ref, par_ref, *rest, nchunk, nh, has_buf):
    o_ref, s_ref, st_scr = rest[1:] if has_buf else rest
    t = pl.program_id(2)
    C = CHUNK

    @pl.when(t == 0)
    def _():
        st_scr[...] = jnp.zeros_like(st_scr)

    tril = _tril_incl(C)
    lane_c = lax.broadcasted_iota(jnp.int32, (SUB, C), 1)
    sub_r = lax.broadcasted_iota(jnp.int32, (SUB, C), 0)

    def chunk(ci, carry):
        rows = pl.ds(pl.multiple_of(ci * C, C), C)
        for j in range(nh):
            cols = slice(j * LANES, (j + 1) * LANES)
            par = par_ref[:, cols]
            q, k, logf = _hgrn_inputs(q_ref[rows, cols], f_ref[rows, cols], par)
            o, st = _hgrn_chunk(q, k, logf, i_ref[rows, cols], st_scr[j], tril, lane_c, sub_r)
            st_scr[j] = st
            o_ref[rows, cols] = (_rms(o, par[3:4]) * _silu(g_ref[rows, cols])).astype(o_ref.dtype)
        return carry

    lax.fori_loop(0, nchunk, chunk, 0)

    @pl.when(t == pl.num_programs(2) - 1)
    def _():
        for j in range(nh):
            s_ref[0, j] = st_scr[j].T


def _hgrn_prompt(pa, par, l, obuf, n_rows, B, T, tb, nh):
    nt = T // tb
    w = nh * LANES
    cb = lambda off: (lambda b, h, t: (b * nt + t, off // w + h))
    ins = [pa, pa, pa, pa, par] + ([obuf] if obuf is not None else [])
    in_specs = [pl.BlockSpec((tb, w), cb(A_HQ)), pl.BlockSpec((tb, w), cb(A_HF)),
                pl.BlockSpec((tb, w), cb(A_HI)), pl.BlockSpec((tb, w), cb(A_HG)),
                _layer_spec((8, w), l, lambda b, h, t: (0, h))] + ([_any_spec()] if obuf is not None else [])
    return pl.pallas_call(
        functools.partial(_hgrn_prompt_kernel, nchunk=tb // CHUNK, nh=nh, has_buf=obuf is not None),
        grid=(B, HEADS // nh, nt), in_specs=in_specs,
        out_specs=[pl.BlockSpec((tb, w), cb(O_HG)),
                   pl.BlockSpec((1, nh, HDIM, HDIM), lambda b, h, t: (b, h, 0, 0))],
        out_shape=[jax.ShapeDtypeStruct((n_rows, D_MODEL), bf16),
                   jax.ShapeDtypeStruct((B, HEADS, HDIM, HDIM), f32)],
        scratch_shapes=[pltpu.VMEM((nh, HDIM, HDIM), f32)],
        input_output_aliases=_alias_last_input(len(ins), obuf is not None),
        compiler_params=_cparams(3), name="hgrn_prompt")(*ins)


def _hgrn_step_kernel(q_ref, f_ref, i_ref, g_ref, par_ref, s_ref, *rest):
    o_ref, so_ref = rest[-2:]
    R = STEP_ROWS
    par = par_ref[...]
    q, k, logf = _hgrn_inputs(q_ref[...], f_ref[...], par)
    v = i_ref[...]
    ef = jnp.exp(logf)
    ones = jnp.ones((R, LANES), bf16)
    row = lax.broadcasted_iota(jnp.int32, (R, LANES), 0)
    o = jnp.zeros((R, LANES), f32)
    for j in range(R):
        so_ref[j, 0] = s_ref[j, 0] * _bcast_cols_tn(_row_mask(ef, j), ones) + _dot_tn(_row_mask(k, j), v)
    for j in range(R):
        o = jnp.where(row == j, _dot(q, so_ref[j, 0]), o)
    o_ref[...] = (_rms(o, par[3:4]) * _silu(g_ref[...])).astype(o_ref.dtype)


def _hgrn_step(pa, par, l, states, obuf, sbuf, row0, nb):
    R = STEP_ROWS
    rb = row0 // R
    cb = lambda off: (lambda i, h: (rb + i, off // LANES + h))
    st_spec = _layer_spec((R, 1, HDIM, HDIM), l, lambda i, h: (i, h, 0, 0))
    ins = [pa, pa, pa, pa, par, states, obuf] + ([sbuf] if sbuf is not None else [])
    aliases = {6: 0, 7: 1} if sbuf is not None else {6: 0}
    return pl.pallas_call(
        _hgrn_step_kernel, grid=(nb // R, HEADS),
        in_specs=[pl.BlockSpec((R, LANES), cb(A_HQ)), pl.BlockSpec((R, LANES), cb(A_HF)),
                  pl.BlockSpec((R, LANES), cb(A_HI)), pl.BlockSpec((R, LANES), cb(A_HG)),
                  _layer_spec((8, LANES), l, lambda i, h: (0, h)), st_spec, _any_spec()]
        + ([_any_spec()] if sbuf is not None else []),
        out_specs=[pl.BlockSpec((R, LANES), cb(O_HG)), st_spec],
        out_shape=[jax.ShapeDtypeStruct(obuf.shape, bf16), jax.ShapeDtypeStruct(states.shape, f32)],
        input_output_aliases=aliases,
        compiler_params=_cparams(2), name="hgrn_step")(*ins)


def _l2norm(t):
    return t * lax.rsqrt(jnp.sum(t * t, axis=-1, keepdims=True) + EPS)


def _gdn_gates(sm, p2, h):
    beta = _extract_col(_sigmoid(sm), S_BETA + h)
    g_all = -jnp.exp(p2[1:2]) * _softplus(sm + p2[0:1])
    return beta, _extract_col(g_all, S_A + h)


def _conv_block(xp_scr, x_ref, w, tb, first):
    @pl.when(first)
    def _():
        xp_scr[0:8, :] = jnp.zeros((8, xp_scr.shape[1]), f32)

    xp_scr[8:8 + tb, :] = x_ref[...]
    out = xp_scr[5:5 + tb, :] * w[0:1]
    for j in range(1, CONV_W):
        out = out + xp_scr[5 + j:5 + j + tb, :] * w[j:j + 1]
    xp_scr[5:8, :] = xp_scr[tb + 5:tb + 8, :]
    return out


def _solve_unit_lower(N, rhs, r, c):
    eye = jnp.where(r == c, 1.0, 0.0)
    blk = (r // TRI) == (c // TRI)
    Nd = jnp.where(blk, N, 0.0)
    No = jnp.where(blk, 0.0, N)
    D = eye + Nd
    P = Nd
    for _ in range(3):
        P = _dot2(P, P)
        yield
        D = D + _dot2(D, P)
        yield
    M = _dot2(D, No)
    Y = _dot2(D, rhs)
    yield
    M2 = _dot2(M, M)
    yield
    W = eye + M
    W = W + _dot2(W, M2)
    yield
    return _dot2(W, Y)


def _gdn_chunk(q, k, v, beta, g, S, tril, r, c):
    C = CHUNK
    gb = jnp.broadcast_to(g, (C, LANES))
    Gb = _sel_dot(tril, gb)
    Dm = _sel_dot(tril, jnp.where(r > c, gb[:, 0:C], 0.0))
    kb = k * beta
    kk = _dot_nt(kb, k)
    qk = _dot_nt(q, k)
    yield
    dec = jnp.where(r >= c, jnp.exp(jnp.minimum(Dm, 0.0)), 0.0)
    eG = jnp.exp(Gb)
    N = jnp.where(r > c, -kk * dec, 0.0)
    X = yield from _solve_unit_lower(N, jnp.concatenate([v * beta, kb * eG], axis=1), r, c)
    yield
    Vn = X[:, 0:HDIM] - _dot(X[:, HDIM:], S)
    o_state = _dot(q * eG, S)
    yield
    o = o_state + _dot(qk * dec, Vn)
    Gl = Gb[C - 1:C]
    return o, S * jnp.exp(Gl) + _dot_tn(k * jnp.exp(Gl - Gb), Vn)


def _round_robin(gens):
    results = [None] * len(gens)
    live = list(range(len(gens)))
    while live:
        for i in list(live):
            try:
                next(gens[i])
            except StopIteration as stop:
                results[i] = stop.value
                live.remove(i)
    return results


def _gdn_prompt_kernel(q_ref, k_ref, v_ref, g_ref, sm_ref, wq_ref, wk_ref, wv_ref, p2_ref, obuf_ref,
                       o_ref, s_ref, xq, xk, xv, cq, ck, cv, s_scr, *, nchunk, tb, nh):
    hb = pl.program_id(1)
    t = pl.program_id(2)
    C = CHUNK
    first = t == 0

    @pl.when(first)
    def _():
        s_scr[...] = jnp.zeros_like(s_scr)

    cq[...] = _silu(_conv_block(xq, q_ref, wq_ref[...], tb, first))
    ck[...] = _silu(_conv_block(xk, k_ref, wk_ref[...], tb, first))
    cv[...] = _silu(_conv_block(xv, v_ref, wv_ref[...], tb, first))

    p2 = p2_ref[...]
    tril = _tril_incl(C)
    r = lax.broadcasted_iota(jnp.int32, (C, C), 0)
    c = lax.broadcasted_iota(jnp.int32, (C, C), 1)

    def chunk(ci, carry):
        rows = pl.ds(pl.multiple_of(ci * C, C), C)
        sm = sm_ref[rows, :]
        gens = []
        for j in range(nh):
            cols = slice(j * LANES, (j + 1) * LANES)
            q = _l2norm(cq[rows, cols]) * (HDIM ** -0.5)
            k = _l2norm(ck[rows, cols])
            beta, g = _gdn_gates(sm, p2, hb * nh + j)
            gens.append(_gdn_chunk(q, k, cv[rows, cols], beta, g, s_scr[j], tril, r, c))
        for j, (o, S) in enumerate(_round_robin(gens)):
            cols = slice(j * LANES, (j + 1) * LANES)
            s_scr[j] = S
            o_ref[rows, cols] = (_rms(o, p2[2:3]) * _silu(g_ref[rows, cols])).astype(o_ref.dtype)
        return carry

    lax.fori_loop(0, nchunk, chunk, 0)

    @pl.when(t == pl.num_programs(2) - 1)
    def _():
        s_ref[0] = s_scr[...]


def _gdn_prompt(pa, pb, ps, pconv, p2, l, obuf, B, T, tb, nh):
    nt = T // tb
    w = nh * LANES
    cb = lambda off: (lambda b, h, t: (b * nt + t, off // w + h))
    wb = lambda off: (lambda b, h, t: (0, off // w + h))
    return pl.pallas_call(
        functools.partial(_gdn_prompt_kernel, nchunk=tb // CHUNK, tb=tb, nh=nh),
        grid=(B, HEADS // nh, nt),
        in_specs=[pl.BlockSpec((tb, w), cb(A_GQKV)), pl.BlockSpec((tb, w), cb(A_GQKV + GDN_W)),
                  pl.BlockSpec((tb, w), cb(A_GQKV + 2 * GDN_W)), pl.BlockSpec((tb, w), cb(B_GG)),
                  pl.BlockSpec((tb, LANES), lambda b, h, t: (b * nt + t, 0)),
                  _layer_spec((8, w), l, wb(0)), _layer_spec((8, w), l, wb(GDN_W)),
                  _layer_spec((8, w), l, wb(2 * GDN_W)),
                  _layer_spec((8, LANES), l, lambda b, h, t: (0, 0)), _any_spec()],
        out_specs=[pl.BlockSpec((tb, w), cb(O_GDN)),
                   pl.BlockSpec((1, nh, HDIM, HDIM), lambda b, h, t: (b, h, 0, 0))],
        out_shape=[jax.ShapeDtypeStruct(obuf.shape, bf16),
                   jax.ShapeDtypeStruct((B, HEADS, HDIM, HDIM), f32)],
        scratch_shapes=[pltpu.VMEM((tb + 8, w), f32)] * 3 + [pltpu.VMEM((tb, w), f32)] * 3
        + [pltpu.VMEM((nh, HDIM, HDIM), f32)],
        input_output_aliases={9: 0},
        compiler_params=_cparams(3), name="gdn_prompt")(pa, pa, pa, pb, ps, pconv, pconv, pconv, p2, obuf)


def _conv_step(x, b0, b1, b2, w):
    return b0 * w[0:1] + b1 * w[1:2] + b2 * w[2:3] + x * w[3:4]


def _gdn_step_kernel(q_ref, k_ref, v_ref, g_ref, sm_ref, q0, q1, q2, k0, k1, k2, v0, v1, v2,
                     wq_ref, wk_ref, wv_ref, p2_ref, s_ref, *rest):
    o_ref, so_ref = rest[-2:]
    R = STEP_ROWS
    h = pl.program_id(1)
    p2 = p2_ref[...]
    q = _l2norm(_silu(_conv_step(q_ref[...], q0[...], q1[...], q2[...], wq_ref[...]))) * (HDIM ** -0.5)
    k = _l2norm(_silu(_conv_step(k_ref[...], k0[...], k1[...], k2[...], wk_ref[...])))
    v = _silu(_conv_step(v_ref[...], v0[...], v1[...], v2[...], wv_ref[...]))
    beta, g = _gdn_gates(sm_ref[...], p2, h)
    eg = jnp.exp(jnp.broadcast_to(g, (R, LANES)))
    row = lax.broadcasted_iota(jnp.int32, (R, LANES), 0)
    qs = jnp.zeros((R, LANES), f32)
    ks = jnp.zeros((R, LANES), f32)
    for j in range(R):
        s = s_ref[j, 0]
        qs = jnp.where(row == j, _dot(q, s), qs)
        ks = jnp.where(row == j, _dot(k, s), ks)
    vn = beta * v - (beta * eg) * ks
    o = eg * qs + jnp.sum(q * k, axis=-1, keepdims=True) * vn
    for j in range(R):
        so_ref[j, 0] = (s_ref[j, 0] * jnp.broadcast_to(eg[j:j + 1, 0:1], (HDIM, HDIM))
                        + _dot_tn(_row_mask(k, j), vn))
    o_ref[...] = (_rms(o, p2[2:3]) * _silu(g_ref[...])).astype(o_ref.dtype)


def _gdn_step(pa, pb, ps, conv_t, pconv, p2, l, states, obuf, sbuf, row0, nb):
    R = STEP_ROWS
    rb = row0 // R
    cb = lambda off: (lambda i, h: (rb + i, off // LANES + h))
    wb = lambda off: (lambda i, h: (0, off // LANES + h))
    tap = lambda off, j: pl.BlockSpec((None, None, R, LANES), lambda i, h: (l, j, i, off // LANES + h))
    taps = [tap(off, j) for off in (0, GDN_W, 2 * GDN_W) for j in range(CONV_W - 1)]
    st_spec = _layer_spec((R, 1, HDIM, HDIM), l, lambda i, h: (i, h, 0, 0))
    ins = [pa, pa, pa, pb, ps] + [conv_t] * 9 + [pconv, pconv, pconv, p2, states, obuf] + (
        [sbuf] if sbuf is not None else [])
    n = len(ins)
    aliases = {n - 2: 0, n - 1: 1} if sbuf is not None else {n - 1: 0}
    return pl.pallas_call(
        _gdn_step_kernel, grid=(nb // R, HEADS),
        in_specs=[pl.BlockSpec((R, LANES), cb(A_GQKV)), pl.BlockSpec((R, LANES), cb(A_GQKV + GDN_W)),
                  pl.BlockSpec((R, LANES), cb(A_GQKV + 2 * GDN_W)), pl.BlockSpec((R, LANES), cb(B_GG)),
                  pl.BlockSpec((R, LANES), lambda i, h: (rb + i, 0))] + taps
        + [_layer_spec((8, LANES), l, wb(0)), _layer_spec((8, LANES), l, wb(GDN_W)),
           _layer_spec((8, LANES), l, wb(2 * GDN_W)), _layer_spec((8, LANES), l, lambda i, h: (0, 0)),
           st_spec, _any_spec()] + ([_any_spec()] if sbuf is not None else []),
        out_specs=[pl.BlockSpec((R, LANES), cb(O_GDN)), st_spec],
        out_shape=[jax.ShapeDtypeStruct(obuf.shape, bf16), jax.ShapeDtypeStruct(states.shape, f32)],
        input_output_aliases=aliases,
        compiler_params=_cparams(2), name="gdn_step")(*ins)


def _head_select(g, width):
    lane = lax.broadcasted_iota(jnp.int32, (LANES, M2_R * width), 0)
    col = lax.broadcasted_iota(jnp.int32, (LANES, M2_R * width), 1)
    return jnp.where(lane == S_DT + g * M2_R + col // width, 1.0, 0.0).astype(bf16)


def _ssd_chunk(x, Bm, Cm, sm, px, sel, st, y_view, tril, r, cmod):
    C = CHUNK
    P = M2_P
    dt_raw = _dot_sel(sm, sel)
    CB = _dot_nt(Cm, Bm)
    y_state = _dot(Cm, st)
    yield
    dt = _softplus(dt_raw + px[5:6])
    a = dt * -jnp.exp(px[6:7])
    GW = _sel_dot(tril, a)
    Dm = _sel_dot(tril, jnp.where(r > cmod, a, 0.0))
    yield
    dec = jnp.where(r >= cmod, jnp.exp(jnp.minimum(Dm, 0.0)), 0.0)
    xdt = x * dt
    for h in range(M2_R):
        sl = slice(h * P, (h + 1) * P)
        y_view[:, sl] = _dot(CB * dec[:, sl], xdt[:, sl])
    Gl = GW[C - 1:C]
    st_new = st * jnp.exp(Gl) + _dot_tn(Bm, xdt * jnp.exp(Gl - GW))
    yield
    return y_view[...] + y_state * jnp.exp(GW) + x * px[7:8], st_new


def _ssd_prompt_kernel(z_ref, x_ref, b_ref, c_ref, sm_ref, px_ref, pb_ref, pc_ref, obuf_ref,
                       o_ref, s_ref, xx, xb, xc, cx, cb_s, cc, y_scr, st_scr, *, nchunk, tb, ng):
    gb = pl.program_id(1)
    t = pl.program_id(2)
    C = CHUNK
    first = t == 0

    @pl.when(first)
    def _():
        st_scr[...] = jnp.zeros_like(st_scr)

    cx[...] = _silu(_conv_block(xx, x_ref, px_ref[...], tb, first) + px_ref[4:5, :])
    cb_s[...] = _silu(_conv_block(xb, b_ref, pb_ref[...], tb, first) + pb_ref[4:5, :])
    cc[...] = _silu(_conv_block(xc, c_ref, pc_ref[...], tb, first) + pc_ref[4:5, :])

    tril = _tril_incl(C)
    r = lax.broadcasted_iota(jnp.int32, (C, M2_GW), 0)
    cmod = lax.broadcasted_iota(jnp.int32, (C, M2_GW), 1) % M2_P
    sels = [_head_select(gb * ng + i, M2_P) for i in range(ng)]

    def chunk(ci, carry):
        rows = pl.ds(pl.multiple_of(ci * C, C), C)
        sm = sm_ref[rows, :]
        gens = []
        for i in range(ng):
            xc_ = slice(i * M2_GW, (i + 1) * M2_GW)
            nc = slice(i * M2_N, (i + 1) * M2_N)
            gens.append(_ssd_chunk(cx[rows, xc_], cb_s[rows, nc], cc[rows, nc], sm, px_ref[:, xc_], sels[i],
                                   st_scr[i], y_scr.at[:, xc_], tril, r, cmod))
        for i, (y, st) in enumerate(_round_robin(gens)):
            xc_ = slice(i * M2_GW, (i + 1) * M2_GW)
            st_scr[i] = st
            y = y * _silu(z_ref[rows, xc_])
            o_ref[rows, xc_] = _rms(y, px_ref[8:9, xc_]).astype(o_ref.dtype)
        return carry

    lax.fori_loop(0, nchunk, chunk, 0)

    @pl.when(t == pl.num_programs(2) - 1)
    def _():
        for i in range(ng):
            s_ref[0, i * M2_R:(i + 1) * M2_R] = st_scr[i].T.reshape(M2_R, M2_P, M2_N)


def _ssd_prompt(pb, ps, px, pbc, l, obuf, B, T, tb, ng):
    nt = T // tb
    wx = ng * M2_GW
    wn = ng * M2_N
    xs = lambda off: (lambda b, g, t: (b * nt + t, off // wx + g))
    bc = lambda off: (lambda b, g, t: (b * nt + t, off // wn + g))
    return pl.pallas_call(
        functools.partial(_ssd_prompt_kernel, nchunk=tb // CHUNK, tb=tb, ng=ng),
        grid=(B, M2_GROUPS // ng, nt),
        in_specs=[pl.BlockSpec((tb, wx), xs(B_MZ)), pl.BlockSpec((tb, wx), xs(B_MXBC)),
                  pl.BlockSpec((tb, wn), bc(B_MXBC + M2_INNER)),
                  pl.BlockSpec((tb, wn), bc(B_MXBC + M2_INNER + M2_BC)),
                  pl.BlockSpec((tb, LANES), lambda b, g, t: (b * nt + t, 0)),
                  _layer_spec((16, wx), l, lambda b, g, t: (0, g)),
                  _layer_spec((8, wn), l, lambda b, g, t: (0, g)),
                  _layer_spec((8, wn), l, lambda b, g, t: (0, M2_BC // wn + g)), _any_spec()],
        out_specs=[pl.BlockSpec((tb, wx), xs(O_M2)),
                   pl.BlockSpec((1, ng * M2_R, M2_P, M2_N), lambda b, g, t: (b, g, 0, 0))],
        out_shape=[jax.ShapeDtypeStruct(obuf.shape, bf16),
                   jax.ShapeDtypeStruct((B, M2_HEADS, M2_P, M2_N), f32)],
        scratch_shapes=[pltpu.VMEM((tb + 8, wx), f32), pltpu.VMEM((tb + 8, wn), f32),
                        pltpu.VMEM((tb + 8, wn), f32), pltpu.VMEM((tb, wx), f32),
                        pltpu.VMEM((tb, wn), f32), pltpu.VMEM((tb, wn), f32),
                        pltpu.VMEM((CHUNK, wx), f32), pltpu.VMEM((ng, M2_N, M2_GW), f32)],
        input_output_aliases={8: 0},
        compiler_params=_cparams(3), name="ssd_prompt")(pb, pb, pb, pb, ps, px, pbc, pbc, obuf)


def _ssd_step_kernel(z_ref, x_ref, b_ref, c_ref, sm_ref, x0, x1, x2, b0, b1, b2, c0, c1, c2,
                     px_ref, pb_ref, pc_ref, s_ref, *rest):
    o_ref, so_ref, y_scr = rest[-3:]
    R = STEP_ROWS
    g = pl.program_id(1)
    P = M2_P
    px = px_ref[...]
    pb = pb_ref[...]
    pc = pc_ref[...]
    x = _silu(_conv_step(x_ref[...], x0[...], x1[...], x2[...], px) + px[4:5])
    Bm = _silu(_conv_step(b_ref[...], b0[...], b1[...], b2[...], pb) + pb[4:5])
    Cm = _silu(_conv_step(c_ref[...], c0[...], c1[...], c2[...], pc) + pc[4:5])
    dt = _softplus(_dot_sel(sm_ref[...], _head_select(g, P)) + px[5:6])
    ea = jnp.exp(dt * -jnp.exp(px[6:7]))
    xdt = x * dt
    row = lax.broadcasted_iota(jnp.int32, (R, P), 0)
    for h in range(M2_R):
        sl = slice(h * P, (h + 1) * P)
        yh = jnp.zeros((R, P), f32)
        for j in range(R):
            s = s_ref[j, h]
            yh = jnp.where(row == j, _dot_nt(Cm, s), yh)
            scale = jnp.broadcast_to(ea[j:j + 1, h * P:h * P + 1], (P, M2_N))
            so_ref[j, h] = s * scale + _dot_tn(_row_mask(xdt[:, sl], j), Bm)
        y_scr[:, sl] = yh
    y = y_scr[...] * ea + jnp.sum(Cm * Bm, axis=-1, keepdims=True) * xdt + x * px[7:8]
    y = y * _silu(z_ref[...])
    o_ref[...] = _rms(y, px[8:9]).astype(o_ref.dtype)


def _ssd_step(pb, ps, conv_t, px, pbc, l, states, obuf, sbuf, row0, nb):
    R = STEP_ROWS
    rb = row0 // R
    xs = lambda off: (lambda i, g: (rb + i, off // M2_GW + g))
    bc = lambda off: (lambda i, g: (rb + i, off // LANES + g))
    xtap = lambda j: pl.BlockSpec((None, None, R, M2_GW), lambda i, g: (l, j, i, g))
    btap = lambda off, j: pl.BlockSpec((None, None, R, LANES), lambda i, g: (l, j, i, off // LANES + g))
    taps = ([xtap(j) for j in range(3)] + [btap(M2_INNER, j) for j in range(3)]
            + [btap(M2_INNER + M2_BC, j) for j in range(3)])
    st_spec = _layer_spec((R, M2_R, M2_P, M2_N), l, lambda i, g: (i, g, 0, 0))
    ins = [pb, pb, pb, pb, ps] + [conv_t] * 9 + [px, pbc, pbc, states, obuf] + ([sbuf] if sbuf is not None else [])
    n = len(ins)
    aliases = {n - 2: 0, n - 1: 1} if sbuf is not None else {n - 1: 0}
    return pl.pallas_call(
        _ssd_step_kernel, grid=(nb // R, M2_GROUPS),
        in_specs=[pl.BlockSpec((R, M2_GW), xs(B_MZ)), pl.BlockSpec((R, M2_GW), xs(B_MXBC)),
                  pl.BlockSpec((R, LANES), bc(B_MXBC + M2_INNER)),
                  pl.BlockSpec((R, LANES), bc(B_MXBC + M2_INNER + M2_BC)),
                  pl.BlockSpec((R, LANES), lambda i, g: (rb + i, 0))] + taps
        + [_layer_spec((16, M2_GW), l, lambda i, g: (0, g)),
           _layer_spec((8, LANES), l, lambda i, g: (0, g)),
           _layer_spec((8, LANES), l, lambda i, g: (0, M2_GROUPS + g)),
           st_spec, _any_spec()] + ([_any_spec()] if sbuf is not None else []),
        out_specs=[pl.BlockSpec((R, M2_GW), xs(O_M2)), st_spec],
        out_shape=[jax.ShapeDtypeStruct(obuf.shape, bf16), jax.ShapeDtypeStruct(states.shape, f32)],
        scratch_shapes=[pltpu.VMEM((R, M2_GW), f32)],
        input_output_aliases=aliases,
        compiler_params=_cparams(2), name="ssd_step")(*ins)


def _pad_rows(a, rows):
    return jnp.pad(a, ((0, 0), (0, rows - a.shape[1]), (0, 0)))


def _pack_params(hg_lb, hg_onorm, gdn_conv, gdn_A_log, gdn_dt_bias, gdn_onorm,
                 m2_conv_w, m2_conv_b, m2_dt_bias, m2_A_log, m2_D, m2_norm):
    lb = jnp.cumsum(jax.nn.softmax(hg_lb.astype(f32), axis=0), axis=0)
    lb = lb - lb[0]
    hg_par = _pad_rows(jnp.stack([jnp.log(lb), jnp.log1p(-lb), 1.0 - lb,
                                  jnp.tile(hg_onorm, (1, HEADS))], axis=1), 8)
    gdn_pconv = _pad_rows(gdn_conv, 8)
    lane_pad = lambda a: jnp.pad(a, ((0, 0), (S_A, LANES - S_A - HEADS)))
    gdn_p2 = _pad_rows(jnp.stack([lane_pad(gdn_dt_bias), lane_pad(gdn_A_log), gdn_onorm], axis=1), 8)
    rep = lambda a: jnp.repeat(a, M2_P, axis=1)
    ssd_px = _pad_rows(jnp.concatenate(
        [m2_conv_w[:, :, :M2_INNER],
         jnp.stack([m2_conv_b[:, :M2_INNER], rep(m2_dt_bias), rep(m2_A_log), rep(m2_D), m2_norm], axis=1)],
        axis=1), 16)
    ssd_pbc = _pad_rows(jnp.concatenate([m2_conv_w[:, :, M2_INNER:], m2_conv_b[:, None, M2_INNER:]], axis=1), 8)
    return hg_par, gdn_pconv, gdn_p2, ssd_px, ssd_pbc


def _split_w_in(w_in):
    wa = w_in[..., 0:7168].astype(bf16)
    wb = w_in[..., 7184:13328].astype(bf16)
    wc = w_in[..., 13360:25648].astype(bf16)
    ws = jnp.concatenate([w_in[..., 7168:7184], w_in[..., 13328:13360],
                          jnp.zeros(w_in.shape[:-1] + (LANES - 48,), w_in.dtype)], axis=-1).astype(bf16)
    return wa, wb, wc, ws


def _trunk(x_all, n_prompt, B, T, states, w, *, tm, tb, nh):
    st_hg, st_gdn, st_gc, st_ssm, st_sc = states
    n_rows = x_all.shape[0]
    nb = n_rows - n_prompt
    hg_par, gdn_pconv, gdn_p2, ssd_px, ssd_pbc = w["packed"]
    wa, wb_, wc, ws = w["w_in"]
    gc_t = jnp.swapaxes(st_gc, 1, 2)
    sc_t = jnp.swapaxes(st_sc, 1, 2)
    outs = {k: [] for k in ("p_hg", "p_gdn", "p_gc", "p_ssm", "p_sc", "s_gc", "s_sc")}
    s_hg = s_gdn = s_ssm = None
    x = x_all
    last3 = lambda p, c0, c1: jnp.stack([lax.slice(p, (b * T + T - (CONV_W - 1), c0), (b * T + T, c1))
                                         for b in range(B)])
    for l in range(DEPTH):
        h = _rmsnorm(x, w["mix_norm"][l], bf16, tm // 4)
        pa = _mm_plain(h, wa, l, tm, 512)
        pb = _mm_plain(h, wb_, l, tm, 512)
        pc = _mm_plain(h, wc, l, tm, 512)
        ps = _mm_plain(h, ws, l, tm, LANES)
        o, s1p = _hgrn_prompt(pa, hg_par, l, None, n_rows, B, T, tb, nh)
        o, s2p = _gdn_prompt(pa, pb, ps, gdn_pconv, gdn_p2, l, o, B, T, tb, nh)
        o, s3p = _ssd_prompt(pb, ps, ssd_px, ssd_pbc, l, o, B, T, tb // 2, 2)
        o, s_hg = _hgrn_step(pa, hg_par, l, st_hg, o, s_hg, n_prompt, nb)
        o, s_gdn = _gdn_step(pa, pb, ps, gc_t, gdn_pconv, gdn_p2, l, st_gdn, o, s_gdn, n_prompt, nb)
        o, s_ssm = _ssd_step(pb, ps, sc_t, ssd_px, ssd_pbc, l, st_ssm, o, s_ssm, n_prompt, nb)
        merged = _merge(o, w["w_branch"], l, pc, tm, 512)
        x = _mm_resid(merged, w["w_out"], l, x, tm, 512)
        h2 = _rmsnorm(x, w["ffn_norm"][l], bf16, tm // 4)
        act = _mm_swiglu(h2, w["w_ffn_in"], l, tm, 256)
        x = _mm_resid(act, w["w_ffn_out"], l, x, tm, 256)
        gq0, gq1 = A_GQKV, A_GQKV + 3 * GDN_W
        mx0, mx1 = B_MXBC, B_MXBC + M2_CONV
        outs["p_hg"].append(s1p)
        outs["p_gdn"].append(s2p)
        outs["p_gc"].append(last3(pa, gq0, gq1))
        outs["p_ssm"].append(s3p)
        outs["p_sc"].append(last3(pb, mx0, mx1))
        outs["s_gc"].append(jnp.concatenate(
            [st_gc[l][:, 1:], lax.slice(pa, (n_prompt, gq0), (n_rows, gq1))[:, None]], axis=1))
        outs["s_sc"].append(jnp.concatenate(
            [st_sc[l][:, 1:], lax.slice(pb, (n_prompt, mx0), (n_rows, mx1))[:, None]], axis=1))
    y = _rmsnorm(x, w["final_norm"], f32, tm // 4)
    o = {k: jnp.stack(v) for k, v in outs.items()}
    o.update(s_hg=s_hg, s_gdn=s_gdn, s_ssm=s_ssm)
    return y, o


def kernel(x_prompt, x_sample, state_hgrn, state_gdn, state_gdn_conv, state_ssm, state_ssm_conv, mix_norm, w_in, hg_lb, hg_onorm, gdn_conv, gdn_A_log, gdn_dt_bias, gdn_onorm, m2_conv_w, m2_conv_b, m2_dt_bias, m2_A_log, m2_D, m2_norm, w_branch, w_out, ffn_norm, w_ffn_in, w_ffn_out, final_norm):
    B, T, D = x_prompt.shape
    nb = x_sample.shape[0]
    w = {"mix_norm": mix_norm, "ffn_norm": ffn_norm, "final_norm": final_norm,
         "w_in": _split_w_in(w_in), "w_branch": w_branch.astype(bf16), "w_out": w_out.astype(bf16),
         "w_ffn_in": w_ffn_in.astype(bf16), "w_ffn_out": w_ffn_out.astype(bf16),
         "packed": _pack_params(hg_lb, hg_onorm, gdn_conv, gdn_A_log, gdn_dt_bias, gdn_onorm,
                                m2_conv_w, m2_conv_b, m2_dt_bias, m2_A_log, m2_D, m2_norm)}
    x_all = jnp.concatenate([x_prompt.reshape(B * T, D), x_sample.reshape(nb, D)], axis=0)
    states = (state_hgrn, state_gdn, state_gdn_conv, state_ssm, state_ssm_conv)
    y, o = _trunk(x_all, B * T, B, T, states, w, tm=832, tb=512, nh=8)
    return (y[:B * T].reshape(B, T, D), y[B * T:].reshape(nb, 1, D),
            o["p_hg"], o["p_gdn"], o["p_gc"], o["p_ssm"], o["p_sc"],
            o["s_hg"], o["s_gdn"], o["s_gc"], o["s_ssm"], o["s_sc"])
```

```python
import functools

import jax
import jax.numpy as jnp
from jax import lax
from jax.experimental import pallas as pl
from jax.experimental.pallas import tpu as pltpu

f32 = jnp.float32
bf16 = jnp.bfloat16

D_MODEL = 4096
DEPTH = 4
HEADS = 8
HDIM = 128
HG_W = HEADS * HDIM
GDN_W = HEADS * HDIM
M2_INNER = D_MODEL // 2
M2_P = 64
M2_HEADS = M2_INNER // M2_P
M2_GROUPS = 4
M2_N = 128
M2_R = M2_HEADS // M2_GROUPS
M2_GW = M2_INNER // M2_GROUPS
M2_BC = M2_GROUPS * M2_N
M2_CONV = M2_INNER + 2 * M2_BC
CONV_W = 4
EPS = 1e-6
CHUNK = 64
SUB = 8
TRI = 16
LANES = 128
STEP_ROWS = 16
VMEM_LIMIT = 56 * 1024 * 1024

A_HQ, A_HF, A_HI, A_HG, A_GQKV = 0, 1024, 2048, 3072, 4096
B_GG, B_MZ, B_MXBC = 0, 1024, 3072
S_BETA, S_A, S_DT = 0, 8, 16
O_HG, O_GDN, O_M2 = 0, HG_W, HG_W + GDN_W


def _cparams(n_axes):
    return pltpu.CompilerParams(dimension_semantics=("arbitrary",) * n_axes,
                                vmem_limit_bytes=VMEM_LIMIT)


def _sigmoid(x):
    return 1.0 / (1.0 + jnp.exp(-x))


def _silu(x):
    return x * _sigmoid(x)


def _softplus(x):
    return jnp.maximum(x, 0.0) + jnp.log1p(jnp.exp(-jnp.abs(x)))


def _log_sigmoid(x):
    return jnp.minimum(x, 0.0) - jnp.log1p(jnp.exp(-jnp.abs(x)))


def _mxu(a, b, dims):
    return lax.dot_general(a, b, (dims, ((), ())), preferred_element_type=f32)


def _dot(a, b):
    return _mxu(a.astype(bf16), b.astype(bf16), ((1,), (0,)))


def _dot_nt(a, b):
    return _mxu(a.astype(bf16), b.astype(bf16), ((1,), (1,)))


def _dot_tn(a, b):
    return _mxu(a.astype(bf16), b.astype(bf16), ((0,), (0,)))


def _split2(x):
    x1 = x.astype(bf16)
    return x1, (x - x1.astype(f32)).astype(bf16)


def _split3(x):
    x1 = x.astype(bf16)
    r = x - x1.astype(f32)
    x2 = r.astype(bf16)
    return x1, x2, (r - x2.astype(f32)).astype(bf16)


def _dot2(a, b):
    a1, a2 = _split2(a)
    b1, b2 = _split2(b)
    d = lambda x, y: _mxu(x, y, ((1,), (0,)))
    return d(a1, b1) + (d(a1, b2) + d(a2, b1))


def _sel_dot(sel, x):
    return sum(_mxu(sel, t, ((1,), (0,))) for t in _split3(x))


def _dot_sel(x, sel):
    return sum(_mxu(t, sel, ((1,), (0,))) for t in _split3(x))


def _bcast_cols_tn(x_masked, ones):
    return sum(_mxu(t, ones, ((0,), (0,))) for t in _split3(x_masked))


def _tril_incl(n):
    r = lax.broadcasted_iota(jnp.int32, (n, n), 0)
    c = lax.broadcasted_iota(jnp.int32, (n, n), 1)
    return jnp.where(r >= c, 1.0, 0.0).astype(bf16)


def _rms(x, w):
    return x * lax.rsqrt(jnp.mean(x * x, axis=-1, keepdims=True) + EPS) * w


def _extract_col(x, lane_idx):
    lane = lax.broadcasted_iota(jnp.int32, x.shape, 1)
    return jnp.sum(jnp.where(lane == lane_idx, x, 0.0), axis=-1, keepdims=True)


def _row_mask(x, j):
    r = lax.broadcasted_iota(jnp.int32, x.shape, 0)
    return jnp.where(r == j, x, 0.0)


def _layer_spec(block, l, index_fn):
    return pl.BlockSpec((None,) + block, lambda *g: (l,) + index_fn(*g))


def _any_spec():
    return pl.BlockSpec(memory_space=pl.ANY)


def _rmsnorm_kernel(x_ref, w_ref, o_ref):
    o_ref[...] = _rms(x_ref[...], w_ref[...]).astype(o_ref.dtype)


def _rmsnorm(x, w, out_dtype, tr):
    m, d = x.shape
    return pl.pallas_call(
        _rmsnorm_kernel, grid=(m // tr,),
        in_specs=[pl.BlockSpec((tr, d), lambda i: (i, 0)), pl.BlockSpec((1, d), lambda i: (0, 0))],
        out_specs=pl.BlockSpec((tr, d), lambda i: (i, 0)),
        out_shape=jax.ShapeDtypeStruct((m, d), out_dtype),
        compiler_params=_cparams(1), name="rmsnorm")(x, w.reshape(1, d))


def _wdot(x, w):
    return jnp.dot(x, w.astype(bf16), preferred_element_type=f32)


def _mm_plain_kernel(x_ref, w_ref, o_ref):
    o_ref[...] = _wdot(x_ref[...], w_ref[...])


def _mm_plain(x, w, l, col0, n, tm, tn):
    m, k = x.shape
    c0 = col0 // tn
    return pl.pallas_call(
        _mm_plain_kernel, grid=(m // tm, n // tn),
        in_specs=[pl.BlockSpec((tm, k), lambda i, j: (i, 0)), _layer_spec((k, tn), l, lambda i, j: (0, c0 + j))],
        out_specs=pl.BlockSpec((tm, tn), lambda i, j: (i, j)),
        out_shape=jax.ShapeDtypeStruct((m, n), f32),
        compiler_params=_cparams(2), name="mm_plain")(x, w)


def _mm_shift_kernel(x_ref, lo_ref, hi_ref, o_ref, *, shift):
    tn = lo_ref.shape[1]
    w = jnp.concatenate([lo_ref[...], hi_ref[...]], axis=1)[:, shift:shift + tn]
    o_ref[...] = _wdot(x_ref[...], w)


def _mm_shift(x, w, l, col0, n, tm, tn):
    m, k = x.shape
    base = (col0 // tn) * tn
    shift = col0 - base
    assert shift < LANES and base % tn == 0
    lo0 = base // tn
    hi0 = (base + tn) // LANES
    step = tn // LANES
    return pl.pallas_call(
        functools.partial(_mm_shift_kernel, shift=shift), grid=(m // tm, n // tn),
        in_specs=[pl.BlockSpec((tm, k), lambda i, j: (i, 0)),
                  _layer_spec((k, tn), l, lambda i, j: (0, lo0 + j)),
                  _layer_spec((k, LANES), l, lambda i, j: (0, hi0 + step * j))],
        out_specs=pl.BlockSpec((tm, tn), lambda i, j: (i, j)),
        out_shape=jax.ShapeDtypeStruct((m, n), f32),
        compiler_params=_cparams(2), name="mm_shift")(x, w, w)


def _mm_small_kernel(x_ref, w1_ref, w2_ref, o_ref):
    lane = lax.broadcasted_iota(jnp.int32, w1_ref.shape, 1)
    w = jnp.where(lane < S_DT, w1_ref[...], jnp.where(lane < S_DT + M2_HEADS, w2_ref[...], 0.0))
    o_ref[...] = _wdot(x_ref[...], w)


def _mm_small(x, w, l, tm):
    m, k = x.shape
    b1, b2 = 7168 // LANES, 13328 // LANES
    assert 7168 % LANES == 0 and 13328 % LANES == S_DT
    return pl.pallas_call(
        _mm_small_kernel, grid=(m // tm,),
        in_specs=[pl.BlockSpec((tm, k), lambda i: (i, 0)), _layer_spec((k, LANES), l, lambda i: (0, b1)),
                  _layer_spec((k, LANES), l, lambda i: (0, b2))],
        out_specs=pl.BlockSpec((tm, LANES), lambda i: (i, 0)),
        out_shape=jax.ShapeDtypeStruct((m, LANES), f32),
        compiler_params=_cparams(1), name="mm_small")(x, w, w)


def _mm_resid_kernel(x_ref, w_ref, r_ref, o_ref):
    o_ref[...] = r_ref[...] + _wdot(x_ref[...], w_ref[...])


def _mm_resid(x, w, l, r, tm, tn):
    m, k = x.shape
    n = w.shape[2]
    return pl.pallas_call(
        _mm_resid_kernel, grid=(m // tm, n // tn),
        in_specs=[pl.BlockSpec((tm, k), lambda i, j: (i, 0)), _layer_spec((k, tn), l, lambda i, j: (0, j)),
                  pl.BlockSpec((tm, tn), lambda i, j: (i, j))],
        out_specs=pl.BlockSpec((tm, tn), lambda i, j: (i, j)),
        out_shape=jax.ShapeDtypeStruct((m, n), f32),
        compiler_params=_cparams(2), name="mm_resid")(x, w, r)


def _mm_swiglu_kernel(x_ref, wg_ref, wu_ref, o_ref):
    x = x_ref[...]
    g = _wdot(x, wg_ref[...])
    u = _wdot(x, wu_ref[...])
    o_ref[...] = (_silu(g) * u).astype(o_ref.dtype)


def _mm_swiglu(x, w, l, tm, tn):
    m, k = x.shape
    n = w.shape[2] // 2
    nb = n // tn
    return pl.pallas_call(
        _mm_swiglu_kernel, grid=(m // tm, nb),
        in_specs=[pl.BlockSpec((tm, k), lambda i, j: (i, 0)), _layer_spec((k, tn), l, lambda i, j: (0, j)),
                  _layer_spec((k, tn), l, lambda i, j: (0, j + nb))],
        out_specs=pl.BlockSpec((tm, tn), lambda i, j: (i, j)),
        out_shape=jax.ShapeDtypeStruct((m, n), bf16),
        compiler_params=_cparams(2), name="mm_swiglu")(x, w, w)


def _merge_kernel(o_ref, w_ref, ga_ref, gb_ref, gc_ref, out_ref):
    a = _wdot(o_ref[:, O_HG:O_GDN], w_ref[O_HG:O_GDN, :])
    b = _wdot(o_ref[:, O_GDN:O_M2], w_ref[O_GDN:O_M2, :])
    c = _wdot(o_ref[:, O_M2:], w_ref[O_M2:, :])
    out = _sigmoid(ga_ref[...]) * a + _sigmoid(gb_ref[...]) * b + _sigmoid(gc_ref[...]) * c
    out_ref[...] = out.astype(out_ref.dtype)


def _merge(o, wb, l, gates, tm, tn):
    m, k = o.shape
    n = wb.shape[2]
    gs = D_MODEL // tn
    return pl.pallas_call(
        _merge_kernel, grid=(m // tm, n // tn),
        in_specs=[pl.BlockSpec((tm, k), lambda i, j: (i, 0)), _layer_spec((k, tn), l, lambda i, j: (0, j)),
                  pl.BlockSpec((tm, tn), lambda i, j: (i, j)),
                  pl.BlockSpec((tm, tn), lambda i, j: (i, gs + j)),
                  pl.BlockSpec((tm, tn), lambda i, j: (i, 2 * gs + j))],
        out_specs=pl.BlockSpec((tm, tn), lambda i, j: (i, j)),
        out_shape=jax.ShapeDtypeStruct((m, n), bf16),
        compiler_params=_cparams(2), name="merge")(o, wb, gates, gates, gates)


def _alias_last_input(n_in, has_buf):
    return {n_in - 1: 0} if has_buf else {}


def _hgrn_inputs(qz, z, par):
    q = _silu(qz)
    a = par[0:1]
    b = par[1:2] + _log_sigmoid(z)
    m = jnp.maximum(a, b)
    logf = m + jnp.log(jnp.exp(a - m) + jnp.exp(b - m))
    k = par[2:3] / (1.0 + jnp.exp(z))
    return q, k, logf


def _hgrn_chunk(q, k, logf, v, st, tril, lane_c, sub_r):
    C = CHUNK
    G = _sel_dot(tril, logf)
    o = _dot_nt(q * jnp.exp(G), st)
    a_rows = []
    for a in range(C // SUB):
        lo = a * SUB
        Ga, qa, ka = G[lo:lo + SUB], q[lo:lo + SUB], k[lo:lo + SUB]
        if a > 0:
            Gs = G[lo - 1:lo]
            qt = qa * jnp.exp(Ga - Gs)
            kt = k * jnp.exp(jnp.minimum(Gs - G, 0.0))
            R = jnp.where(lane_c < lo, _dot_nt(qt, kt), 0.0)
        else:
            R = jnp.zeros((SUB, C), f32)
        for jl in range(SUB):
            e = qa * ka[jl:jl + 1] * jnp.exp(jnp.minimum(Ga - Ga[jl:jl + 1], 0.0))
            c = jnp.sum(e, axis=-1, keepdims=True)
            R = jnp.where(lane_c == lo + jl, jnp.where(sub_r >= jl, c, 0.0), R)
        a_rows.append(R)
    A = jnp.concatenate(a_rows, axis=0)
    o = o + _dot(A, v)
    Gl = G[C - 1:C]
    return o, st * jnp.exp(Gl) + _dot_tn(v, k * jnp.exp(Gl - G))


def _hgrn_prompt_kernel(q_ref, f_ref, i_ref, g_ref, par_ref, *rest, nchunk, nh, has_buf):
    o_ref, s_ref, st_scr = rest[1:] if has_buf else rest
    t = pl.program_id(2)
    C = CHUNK

    @pl.when(t == 0)
    def _():
        st_scr[...] = jnp.zeros_like(st_scr)

    tril = _tril_incl(C)
    lane_c = lax.broadcasted_iota(jnp.int32, (SUB, C), 1)
    sub_r = lax.broadcasted_iota(jnp.int32, (SUB, C), 0)

    def chunk(ci, carry):
        rows = pl.ds(pl.multiple_of(ci * C, C), C)
        for j in range(nh):
            cols = slice(j * LANES, (j + 1) * LANES)
            par = par_ref[:, cols]
            q, k, logf = _hgrn_inputs(q_ref[rows, cols], f_ref[rows, cols], par)
            o, st = _hgrn_chunk(q, k, logf, i_ref[rows, cols], st_scr[j], tril, lane_c, sub_r)
            st_scr[j] = st
            o_ref[rows, cols] = (_rms(o, par[3:4]) * _silu(g_ref[rows, cols])).astype(o_ref.dtype)
        return carry

    lax.fori_loop(0, nchunk, chunk, 0)

    @pl.when(t == pl.num_programs(2) - 1)
    def _():
        for j in range(nh):
            s_ref[0, j] = st_scr[j].T


def _hgrn_prompt(pa, par, l, obuf, n_rows, B, T, tb, nh):
    nt = T // tb
    w = nh * LANES
    cb = lambda off: (lambda b, h, t: (b * nt + t, off // w + h))
    ins = [pa, pa, pa, pa, par] + ([obuf] if obuf is not None else [])
    in_specs = [pl.BlockSpec((tb, w), cb(A_HQ)), pl.BlockSpec((tb, w), cb(A_HF)),
                pl.BlockSpec((tb, w), cb(A_HI)), pl.BlockSpec((tb, w), cb(A_HG)),
                _layer_spec((8, w), l, lambda b, h, t: (0, h))] + ([_any_spec()] if obuf is not None else [])
    return pl.pallas_call(
        functools.partial(_hgrn_prompt_kernel, nchunk=tb // CHUNK, nh=nh, has_buf=obuf is not None),
        grid=(B, HEADS // nh, nt), in_specs=in_specs,
        out_specs=[pl.BlockSpec((tb, w), cb(O_HG)),
                   pl.BlockSpec((1, nh, HDIM, HDIM), lambda b, h, t: (b, h, 0, 0))],
        out_shape=[jax.ShapeDtypeStruct((n_rows, D_MODEL), bf16),
                   jax.ShapeDtypeStruct((B, HEADS, HDIM, HDIM), f32)],
        scratch_shapes=[pltpu.VMEM((nh, HDIM, HDIM), f32)],
        input_output_aliases=_alias_last_input(len(ins), obuf is not None),
        compiler_params=_cparams(3), name="hgrn_prompt")(*ins)


def _hgrn_step_kernel(q_ref, f_ref, i_ref, g_ref, par_ref, s_ref, *rest):
    o_ref, so_ref = rest[-2:]
    R = STEP_ROWS
    par = par_ref[...]
    q, k, logf = _hgrn_inputs(q_ref[...], f_ref[...], par)
    v = i_ref[...]
    ef = jnp.exp(logf)
    ones = jnp.ones((R, LANES), bf16)
    row = lax.broadcasted_iota(jnp.int32, (R, LANES), 0)
    o = jnp.zeros((R, LANES), f32)
    for j in range(R):
        so_ref[j, 0] = s_ref[j, 0] * _bcast_cols_tn(_row_mask(ef, j), ones) + _dot_tn(_row_mask(k, j), v)
    for j in range(R):
        o = jnp.where(row == j, _dot(q, so_ref[j, 0]), o)
    o_ref[...] = (_rms(o, par[3:4]) * _silu(g_ref[...])).astype(o_ref.dtype)


def _hgrn_step(pa, par, l, states, obuf, sbuf, row0, nb):
    R = STEP_ROWS
    rb = row0 // R
    cb = lambda off: (lambda i, h: (rb + i, off // LANES + h))
    st_spec = _layer_spec((R, 1, HDIM, HDIM), l, lambda i, h: (i, h, 0, 0))
    ins = [pa, pa, pa, pa, par, states, obuf] + ([sbuf] if sbuf is not None else [])
    aliases = {6: 0, 7: 1} if sbuf is not None else {6: 0}
    return pl.pallas_call(
        _hgrn_step_kernel, grid=(nb // R, HEADS),
        in_specs=[pl.BlockSpec((R, LANES), cb(A_HQ)), pl.BlockSpec((R, LANES), cb(A_HF)),
                  pl.BlockSpec((R, LANES), cb(A_HI)), pl.BlockSpec((R, LANES), cb(A_HG)),
                  _layer_spec((8, LANES), l, lambda i, h: (0, h)), st_spec, _any_spec()]
        + ([_any_spec()] if sbuf is not None else []),
        out_specs=[pl.BlockSpec((R, LANES), cb(O_HG)), st_spec],
        out_shape=[jax.ShapeDtypeStruct(obuf.shape, bf16), jax.ShapeDtypeStruct(states.shape, f32)],
        input_output_aliases=aliases,
        compiler_params=_cparams(2), name="hgrn_step")(*ins)


def _l2norm(t):
    return t * lax.rsqrt(jnp.sum(t * t, axis=-1, keepdims=True) + EPS)


def _gdn_gates(sm, p2, h):
    beta = _extract_col(_sigmoid(sm), S_BETA + h)
    g_all = -jnp.exp(p2[1:2]) * _softplus(sm + p2[0:1])
    return beta, _extract_col(g_all, S_A + h)


def _conv_block(xp_scr, x_ref, w, tb, first):
    @pl.when(first)
    def _():
        xp_scr[0:8, :] = jnp.zeros((8, xp_scr.shape[1]), f32)

    xp_scr[8:8 + tb, :] = x_ref[...]
    out = xp_scr[5:5 + tb, :] * w[0:1]
    for j in range(1, CONV_W):
        out = out + xp_scr[5 + j:5 + j + tb, :] * w[j:j + 1]
    xp_scr[5:8, :] = xp_scr[tb + 5:tb + 8, :]
    return out


def _solve_unit_lower(N, rhs, r, c):
    eye = jnp.where(r == c, 1.0, 0.0)
    blk = (r // TRI) == (c // TRI)
    Nd = jnp.where(blk, N, 0.0)
    No = jnp.where(blk, 0.0, N)
    D = eye + Nd
    P = Nd
    for _ in range(3):
        P = _dot2(P, P)
        yield
        D = D + _dot2(D, P)
        yield
    M = _dot2(D, No)
    Y = _dot2(D, rhs)
    yield
    M2 = _dot2(M, M)
    yield
    W = eye + M
    W = W + _dot2(W, M2)
    yield
    return _dot2(W, Y)


def _gdn_chunk(q, k, v, beta, g, S, tril, r, c):
    C = CHUNK
    gb = jnp.broadcast_to(g, (C, LANES))
    Gb = _sel_dot(tril, gb)
    Dm = _sel_dot(tril, jnp.where(r > c, gb[:, 0:C], 0.0))
    kb = k * beta
    kk = _dot_nt(kb, k)
    qk = _dot_nt(q, k)
    yield
    dec = jnp.where(r >= c, jnp.exp(jnp.minimum(Dm, 0.0)), 0.0)
    eG = jnp.exp(Gb)
    N = jnp.where(r > c, -kk * dec, 0.0)
    X = yield from _solve_unit_lower(N, jnp.concatenate([v * beta, kb * eG], axis=1), r, c)
    yield
    Vn = X[:, 0:HDIM] - _dot(X[:, HDIM:], S)
    o_state = _dot(q * eG, S)
    yield
    o = o_state + _dot(qk * dec, Vn)
    Gl = Gb[C - 1:C]
    return o, S * jnp.exp(Gl) + _dot_tn(k * jnp.exp(Gl - Gb), Vn)


def _round_robin(gens):
    results = [None] * len(gens)
    live = list(range(len(gens)))
    while live:
        for i in list(live):
            try:
                next(gens[i])
            except StopIteration as stop:
                results[i] = stop.value
                live.remove(i)
    return results


def _gdn_prompt_kernel(q_ref, k_ref, v_ref, g_ref, sm_ref, wq_ref, wk_ref, wv_ref, p2_ref, obuf_ref,
                       o_ref, s_ref, xq, xk, xv, cq, ck, cv, s_scr, *, nchunk, tb, nh):
    hb = pl.program_id(1)
    t = pl.program_id(2)
    C = CHUNK
    first = t == 0

    @pl.when(first)
    def _():
        s_scr[...] = jnp.zeros_like(s_scr)

    cq[...] = _silu(_conv_block(xq, q_ref, wq_ref[...], tb, first))
    ck[...] = _silu(_conv_block(xk, k_ref, wk_ref[...], tb, first))
    cv[...] = _silu(_conv_block(xv, v_ref, wv_ref[...], tb, first))

    p2 = p2_ref[...]
    tril = _tril_incl(C)
    r = lax.broadcasted_iota(jnp.int32, (C, C), 0)
    c = lax.broadcasted_iota(jnp.int32, (C, C), 1)

    def chunk(ci, carry):
        rows = pl.ds(pl.multiple_of(ci * C, C), C)
        sm = sm_ref[rows, :]
        gens = []
        for j in range(nh):
            cols = slice(j * LANES, (j + 1) * LANES)
            q = _l2norm(cq[rows, cols]) * (HDIM ** -0.5)
            k = _l2norm(ck[rows, cols])
            beta, g = _gdn_gates(sm, p2, hb * nh + j)
            gens.append(_gdn_chunk(q, k, cv[rows, cols], beta, g, s_scr[j], tril, r, c))
        for j, (o, S) in enumerate(_round_robin(gens)):
            cols = slice(j * LANES, (j + 1) * LANES)
            s_scr[j] = S
            o_ref[rows, cols] = (_rms(o, p2[2:3]) * _silu(g_ref[rows, cols])).astype(o_ref.dtype)
        return carry

    lax.fori_loop(0, nchunk, chunk, 0)

    @pl.when(t == pl.num_programs(2) - 1)
    def _():
        s_ref[0] = s_scr[...]


def _gdn_prompt(pa, pb, ps, pconv, p2, l, obuf, B, T, tb, nh):
    nt = T // tb
    w = nh * LANES
    cb = lambda off: (lambda b, h, t: (b * nt + t, off // w + h))
    wb = lambda off: (lambda b, h, t: (0, off // w + h))
    return pl.pallas_call(
        functools.partial(_gdn_prompt_kernel, nchunk=tb // CHUNK, tb=tb, nh=nh),
        grid=(B, HEADS // nh, nt),
        in_specs=[pl.BlockSpec((tb, w), cb(A_GQKV)), pl.BlockSpec((tb, w), cb(A_GQKV + GDN_W)),
                  pl.BlockSpec((tb, w), cb(A_GQKV + 2 * GDN_W)), pl.BlockSpec((tb, w), cb(B_GG)),
                  pl.BlockSpec((tb, LANES), lambda b, h, t: (b * nt + t, 0)),
                  _layer_spec((8, w), l, wb(0)), _layer_spec((8, w), l, wb(GDN_W)),
                  _layer_spec((8, w), l, wb(2 * GDN_W)),
                  _layer_spec((8, LANES), l, lambda b, h, t: (0, 0)), _any_spec()],
        out_specs=[pl.BlockSpec((tb, w), cb(O_GDN)),
                   pl.BlockSpec((1, nh, HDIM, HDIM), lambda b, h, t: (b, h, 0, 0))],
        out_shape=[jax.ShapeDtypeStruct(obuf.shape, bf16),
                   jax.ShapeDtypeStruct((B, HEADS, HDIM, HDIM), f32)],
        scratch_shapes=[pltpu.VMEM((tb + 8, w), f32)] * 3 + [pltpu.VMEM((tb, w), f32)] * 3
        + [pltpu.VMEM((nh, HDIM, HDIM), f32)],
        input_output_aliases={9: 0},
        compiler_params=_cparams(3), name="gdn_prompt")(pa, pa, pa, pb, ps, pconv, pconv, pconv, p2, obuf)


def _conv_step(x, b0, b1, b2, w):
    return b0 * w[0:1] + b1 * w[1:2] + b2 * w[2:3] + x * w[3:4]


def _gdn_step_kernel(q_ref, k_ref, v_ref, g_ref, sm_ref, q0, q1, q2, k0, k1, k2, v0, v1, v2,
                     wq_ref, wk_ref, wv_ref, p2_ref, s_ref, *rest):
    o_ref, so_ref = rest[-2:]
    R = STEP_ROWS
    h = pl.program_id(1)
    p2 = p2_ref[...]
    q = _l2norm(_silu(_conv_step(q_ref[...], q0[...], q1[...], q2[...], wq_ref[...]))) * (HDIM ** -0.5)
    k = _l2norm(_silu(_conv_step(k_ref[...], k0[...], k1[...], k2[...], wk_ref[...])))
    v = _silu(_conv_step(v_ref[...], v0[...], v1[...], v2[...], wv_ref[...]))
    beta, g = _gdn_gates(sm_ref[...], p2, h)
    eg = jnp.exp(jnp.broadcast_to(g, (R, LANES)))
    row = lax.broadcasted_iota(jnp.int32, (R, LANES), 0)
    qs = jnp.zeros((R, LANES), f32)
    ks = jnp.zeros((R, LANES), f32)
    for j in range(R):
        s = s_ref[j, 0]
        qs = jnp.where(row == j, _dot(q, s), qs)
        ks = jnp.where(row == j, _dot(k, s), ks)
    vn = beta * v - (beta * eg) * ks
    o = eg * qs + jnp.sum(q * k, axis=-1, keepdims=True) * vn
    for j in range(R):
        so_ref[j, 0] = (s_ref[j, 0] * jnp.broadcast_to(eg[j:j + 1, 0:1], (HDIM, HDIM))
                        + _dot_tn(_row_mask(k, j), vn))
    o_ref[...] = (_rms(o, p2[2:3]) * _silu(g_ref[...])).astype(o_ref.dtype)


def _gdn_step(pa, pb, ps, conv_t, pconv, p2, l, states, obuf, sbuf, row0, nb):
    R = STEP_ROWS
    rb = row0 // R
    cb = lambda off: (lambda i, h: (rb + i, off // LANES + h))
    wb = lambda off: (lambda i, h: (0, off // LANES + h))
    tap = lambda off, j: pl.BlockSpec((None, None, R, LANES), lambda i, h: (l, j, i, off // LANES + h))
    taps = [tap(off, j) for off in (0, GDN_W, 2 * GDN_W) for j in range(CONV_W - 1)]
    st_spec = _layer_spec((R, 1, HDIM, HDIM), l, lambda i, h: (i, h, 0, 0))
    ins = [pa, pa, pa, pb, ps] + [conv_t] * 9 + [pconv, pconv, pconv, p2, states, obuf] + (
        [sbuf] if sbuf is not None else [])
    n = len(ins)
    aliases = {n - 2: 0, n - 1: 1} if sbuf is not None else {n - 1: 0}
    return pl.pallas_call(
        _gdn_step_kernel, grid=(nb // R, HEADS),
        in_specs=[pl.BlockSpec((R, LANES), cb(A_GQKV)), pl.BlockSpec((R, LANES), cb(A_GQKV + GDN_W)),
                  pl.BlockSpec((R, LANES), cb(A_GQKV + 2 * GDN_W)), pl.BlockSpec((R, LANES), cb(B_GG)),
                  pl.BlockSpec((R, LANES), lambda i, h: (rb + i, 0))] + taps
        + [_layer_spec((8, LANES), l, wb(0)), _layer_spec((8, LANES), l, wb(GDN_W)),
           _layer_spec((8, LANES), l, wb(2 * GDN_W)), _layer_spec((8, LANES), l, lambda i, h: (0, 0)),
           st_spec, _any_spec()] + ([_any_spec()] if sbuf is not None else []),
        out_specs=[pl.BlockSpec((R, LANES), cb(O_GDN)), st_spec],
        out_shape=[jax.ShapeDtypeStruct(obuf.shape, bf16), jax.ShapeDtypeStruct(states.shape, f32)],
        input_output_aliases=aliases,
        compiler_params=_cparams(2), name="gdn_step")(*ins)


def _head_select(g, width):
    lane = lax.broadcasted_iota(jnp.int32, (LANES, M2_R * width), 0)
    col = lax.broadcasted_iota(jnp.int32, (LANES, M2_R * width), 1)
    return jnp.where(lane == S_DT + g * M2_R + col // width, 1.0, 0.0).astype(bf16)


def _ssd_chunk(x, Bm, Cm, sm, px, sel, st, y_view, tril, r, cmod):
    C = CHUNK
    P = M2_P
    dt_raw = _dot_sel(sm, sel)
    CB = _dot_nt(Cm, Bm)
    y_state = _dot(Cm, st)
    yield
    dt = _softplus(dt_raw + px[5:6])
    a = dt * -jnp.exp(px[6:7])
    GW = _sel_dot(tril, a)
    Dm = _sel_dot(tril, jnp.where(r > cmod, a, 0.0))
    yield
    dec = jnp.where(r >= cmod, jnp.exp(jnp.minimum(Dm, 0.0)), 0.0)
    xdt = x * dt
    for h in range(M2_R):
        sl = slice(h * P, (h + 1) * P)
        y_view[:, sl] = _dot(CB * dec[:, sl], xdt[:, sl])
    Gl = GW[C - 1:C]
    st_new = st * jnp.exp(Gl) + _dot_tn(Bm, xdt * jnp.exp(Gl - GW))
    yield
    return y_view[...] + y_state * jnp.exp(GW) + x * px[7:8], st_new


def _ssd_prompt_kernel(z_ref, x_ref, b_ref, c_ref, sm_ref, px_ref, pb_ref, pc_ref, obuf_ref,
                       o_ref, s_ref, xx, xb, xc, cx, cb_s, cc, y_scr, st_scr, *, nchunk, tb, ng):
    gb = pl.program_id(1)
    t = pl.program_id(2)
    C = CHUNK
    first = t == 0

    @pl.when(first)
    def _():
        st_scr[...] = jnp.zeros_like(st_scr)

    cx[...] = _silu(_conv_block(xx, x_ref, px_ref[...], tb, first) + px_ref[4:5, :])
    cb_s[...] = _silu(_conv_block(xb, b_ref, pb_ref[...], tb, first) + pb_ref[4:5, :])
    cc[...] = _silu(_conv_block(xc, c_ref, pc_ref[...], tb, first) + pc_ref[4:5, :])

    tril = _tril_incl(C)
    r = lax.broadcasted_iota(jnp.int32, (C, M2_GW), 0)
    cmod = lax.broadcasted_iota(jnp.int32, (C, M2_GW), 1) % M2_P
    sels = [_head_select(gb * ng + i, M2_P) for i in range(ng)]

    def chunk(ci, carry):
        rows = pl.ds(pl.multiple_of(ci * C, C), C)
        sm = sm_ref[rows, :]
        gens = []
        for i in range(ng):
            xc_ = slice(i * M2_GW, (i + 1) * M2_GW)
            nc = slice(i * M2_N, (i + 1) * M2_N)
            gens.append(_ssd_chunk(cx[rows, xc_], cb_s[rows, nc], cc[rows, nc], sm, px_ref[:, xc_], sels[i],
                                   st_scr[i], y_scr.at[:, xc_], tril, r, cmod))
        for i, (y, st) in enumerate(_round_robin(gens)):
            xc_ = slice(i * M2_GW, (i + 1) * M2_GW)
            st_scr[i] = st
            y = y * _silu(z_ref[rows, xc_])
            o_ref[rows, xc_] = _rms(y, px_ref[8:9, xc_]).astype(o_ref.dtype)
        return carry

    lax.fori_loop(0, nchunk, chunk, 0)

    @pl.when(t == pl.num_programs(2) - 1)
    def _():
        for i in range(ng):
            s_ref[0, i * M2_R:(i + 1) * M2_R] = st_scr[i].T.reshape(M2_R, M2_P, M2_N)


def _ssd_prompt(pb, ps, px, pbc, l, obuf, B, T, tb, ng):
    nt = T // tb
    wx = ng * M2_GW
    wn = ng * M2_N
    xs = lambda off: (lambda b, g, t: (b * nt + t, off // wx + g))
    bc = lambda off: (lambda b, g, t: (b * nt + t, off // wn + g))
    return pl.pallas_call(
        functools.partial(_ssd_prompt_kernel, nchunk=tb // CHUNK, tb=tb, ng=ng),
        grid=(B, M2_GROUPS // ng, nt),
        in_specs=[pl.BlockSpec((tb, wx), xs(B_MZ)), pl.BlockSpec((tb, wx), xs(B_MXBC)),
                  pl.BlockSpec((tb, wn), bc(B_MXBC + M2_INNER)),
                  pl.BlockSpec((tb, wn), bc(B_MXBC + M2_INNER + M2_BC)),
                  pl.BlockSpec((tb, LANES), lambda b, g, t: (b * nt + t, 0)),
                  _layer_spec((16, wx), l, lambda b, g, t: (0, g)),
                  _layer_spec((8, wn), l, lambda b, g, t: (0, g)),
                  _layer_spec((8, wn), l, lambda b, g, t: (0, M2_BC // wn + g)), _any_spec()],
        out_specs=[pl.BlockSpec((tb, wx), xs(O_M2)),
                   pl.BlockSpec((1, ng * M2_R, M2_P, M2_N), lambda b, g, t: (b, g, 0, 0))],
        out_shape=[jax.ShapeDtypeStruct(obuf.shape, bf16),
                   jax.ShapeDtypeStruct((B, M2_HEADS, M2_P, M2_N), f32)],
        scratch_shapes=[pltpu.VMEM((tb + 8, wx), f32), pltpu.VMEM((tb + 8, wn), f32),
                        pltpu.VMEM((tb + 8, wn), f32), pltpu.VMEM((tb, wx), f32),
                        pltpu.VMEM((tb, wn), f32), pltpu.VMEM((tb, wn), f32),
                        pltpu.VMEM((CHUNK, wx), f32), pltpu.VMEM((ng, M2_N, M2_GW), f32)],
        input_output_aliases={8: 0},
        compiler_params=_cparams(3), name="ssd_prompt")(pb, pb, pb, pb, ps, px, pbc, pbc, obuf)


def _ssd_step_kernel(z_ref, x_ref, b_ref, c_ref, sm_ref, x0, x1, x2, b0, b1, b2, c0, c1, c2,
                     px_ref, pb_ref, pc_ref, s_ref, *rest):
    o_ref, so_ref, y_scr = rest[-3:]
    R = STEP_ROWS
    g = pl.program_id(1)
    P = M2_P
    px = px_ref[...]
    pb = pb_ref[...]
    pc = pc_ref[...]
    x = _silu(_conv_step(x_ref[...], x0[...], x1[...], x2[...], px) + px[4:5])
    Bm = _silu(_conv_step(b_ref[...], b0[...], b1[...], b2[...], pb) + pb[4:5])
    Cm = _silu(_conv_step(c_ref[...], c0[...], c1[...], c2[...], pc) + pc[4:5])
    dt = _softplus(_dot_sel(sm_ref[...], _head_select(g, P)) + px[5:6])
    ea = jnp.exp(dt * -jnp.exp(px[6:7]))
    xdt = x * dt
    row = lax.broadcasted_iota(jnp.int32, (R, P), 0)
    for h in range(M2_R):
        sl = slice(h * P, (h + 1) * P)
        yh = jnp.zeros((R, P), f32)
        for j in range(R):
            s = s_ref[j, h]
            yh = jnp.where(row == j, _dot_nt(Cm, s), yh)
            scale = jnp.broadcast_to(ea[j:j + 1, h * P:h * P + 1], (P, M2_N))
            so_ref[j, h] = s * scale + _dot_tn(_row_mask(xdt[:, sl], j), Bm)
        y_scr[:, sl] = yh
    y = y_scr[...] * ea + jnp.sum(Cm * Bm, axis=-1, keepdims=True) * xdt + x * px[7:8]
    y = y * _silu(z_ref[...])
    o_ref[...] = _rms(y, px[8:9]).astype(o_ref.dtype)


def _ssd_step(pb, ps, conv_t, px, pbc, l, states, obuf, sbuf, row0, nb):
    R = STEP_ROWS
    rb = row0 // R
    xs = lambda off: (lambda i, g: (rb + i, off // M2_GW + g))
    bc = lambda off: (lambda i, g: (rb + i, off // LANES + g))
    xtap = lambda j: pl.BlockSpec((None, None, R, M2_GW), lambda i, g: (l, j, i, g))
    btap = lambda off, j: pl.BlockSpec((None, None, R, LANES), lambda i, g: (l, j, i, off // LANES + g))
    taps = ([xtap(j) for j in range(3)] + [btap(M2_INNER, j) for j in range(3)]
            + [btap(M2_INNER + M2_BC, j) for j in range(3)])
    st_spec = _layer_spec((R, M2_R, M2_P, M2_N), l, lambda i, g: (i, g, 0, 0))
    ins = [pb, pb, pb, pb, ps] + [conv_t] * 9 + [px, pbc, pbc, states, obuf] + ([sbuf] if sbuf is not None else [])
    n = len(ins)
    aliases = {n - 2: 0, n - 1: 1} if sbuf is not None else {n - 1: 0}
    return pl.pallas_call(
        _ssd_step_kernel, grid=(nb // R, M2_GROUPS),
        in_specs=[pl.BlockSpec((R, M2_GW), xs(B_MZ)), pl.BlockSpec((R, M2_GW), xs(B_MXBC)),
                  pl.BlockSpec((R, LANES), bc(B_MXBC + M2_INNER)),
                  pl.BlockSpec((R, LANES), bc(B_MXBC + M2_INNER + M2_BC)),
                  pl.BlockSpec((R, LANES), lambda i, g: (rb + i, 0))] + taps
        + [_layer_spec((16, M2_GW), l, lambda i, g: (0, g)),
           _layer_spec((8, LANES), l, lambda i, g: (0, g)),
           _layer_spec((8, LANES), l, lambda i, g: (0, M2_GROUPS + g)),
           st_spec, _any_spec()] + ([_any_spec()] if sbuf is not None else []),
        out_specs=[pl.BlockSpec((R, M2_GW), xs(O_M2)), st_spec],
        out_shape=[jax.ShapeDtypeStruct(obuf.shape, bf16), jax.ShapeDtypeStruct(states.shape, f32)],
        scratch_shapes=[pltpu.VMEM((R, M2_GW), f32)],
        input_output_aliases=aliases,
        compiler_params=_cparams(2), name="ssd_step")(*ins)


def _pad_rows(a, rows):
    return jnp.pad(a, ((0, 0), (0, rows - a.shape[1]), (0, 0)))


def _pack_params(hg_lb, hg_onorm, gdn_conv, gdn_A_log, gdn_dt_bias, gdn_onorm,
                 m2_conv_w, m2_conv_b, m2_dt_bias, m2_A_log, m2_D, m2_norm):
    lb = jnp.cumsum(jax.nn.softmax(hg_lb.astype(f32), axis=0), axis=0)
    lb = lb - lb[0]
    hg_par = _pad_rows(jnp.stack([jnp.log(lb), jnp.log1p(-lb), 1.0 - lb,
                                  jnp.tile(hg_onorm, (1, HEADS))], axis=1), 8)
    gdn_pconv = _pad_rows(gdn_conv, 8)
    lane_pad = lambda a: jnp.pad(a, ((0, 0), (S_A, LANES - S_A - HEADS)))
    gdn_p2 = _pad_rows(jnp.stack([lane_pad(gdn_dt_bias), lane_pad(gdn_A_log), gdn_onorm], axis=1), 8)
    rep = lambda a: jnp.repeat(a, M2_P, axis=1)
    ssd_px = _pad_rows(jnp.concatenate(
        [m2_conv_w[:, :, :M2_INNER],
         jnp.stack([m2_conv_b[:, :M2_INNER], rep(m2_dt_bias), rep(m2_A_log), rep(m2_D), m2_norm], axis=1)],
        axis=1), 16)
    ssd_pbc = _pad_rows(jnp.concatenate([m2_conv_w[:, :, M2_INNER:], m2_conv_b[:, None, M2_INNER:]], axis=1), 8)
    return hg_par, gdn_pconv, gdn_p2, ssd_px, ssd_pbc


def _trunk(x_all, n_prompt, B, T, states, w, *, tm, tm_ffn, tr, tb, nh):
    st_hg, st_gdn, st_gc, st_ssm, st_sc = states
    n_rows = x_all.shape[0]
    nb = n_rows - n_prompt
    hg_par, gdn_pconv, gdn_p2, ssd_px, ssd_pbc = w["packed"]
    w_in = w["w_in"]
    gc_t = jnp.swapaxes(st_gc, 1, 2)
    sc_t = jnp.swapaxes(st_sc, 1, 2)
    outs = {k: [] for k in ("p_hg", "p_gdn", "p_gc", "p_ssm", "p_sc", "s_gc", "s_sc")}
    s_hg = s_gdn = s_ssm = None
    x = x_all
    last3 = lambda p, c0, c1: jnp.stack([lax.slice(p, (b * T + T - (CONV_W - 1), c0), (b * T + T, c1))
                                         for b in range(B)])
    for l in range(DEPTH):
        h = _rmsnorm(x, w["mix_norm"][l], bf16, tr)
        pa = _mm_plain(h, w_in, l, 0, 7168, tm, 256)
        pb = _mm_shift(h, w_in, l, 7184, 6144, tm, 256)
        pc = _mm_shift(h, w_in, l, 13360, 3 * D_MODEL, tm, 256)
        ps = _mm_small(h, w_in, l, tm)
        o, s1p = _hgrn_prompt(pa, hg_par, l, None, n_rows, B, T, tb, nh)
        o, s2p = _gdn_prompt(pa, pb, ps, gdn_pconv, gdn_p2, l, o, B, T, tb, nh)
        o, s3p = _ssd_prompt(pb, ps, ssd_px, ssd_pbc, l, o, B, T, tb // 2, 2)
        o, s_hg = _hgrn_step(pa, hg_par, l, st_hg, o, s_hg, n_prompt, nb)
        o, s_gdn = _gdn_step(pa, pb, ps, gc_t, gdn_pconv, gdn_p2, l, st_gdn, o, s_gdn, n_prompt, nb)
        o, s_ssm = _ssd_step(pb, ps, sc_t, ssd_px, ssd_pbc, l, st_ssm, o, s_ssm, n_prompt, nb)
        merged = _merge(o, w["w_branch"], l, pc, tm, 256)
        x = _mm_resid(merged, w["w_out"], l, x, tm, 256)
        h2 = _rmsnorm(x, w["ffn_norm"][l], bf16, tr)
        act = _mm_swiglu(h2, w["w_ffn_in"], l, tm, 256)
        x = _mm_resid(act, w["w_ffn_out"], l, x, tm_ffn, 256)
        gq0, gq1 = A_GQKV, A_GQKV + 3 * GDN_W
        mx0, mx1 = B_MXBC, B_MXBC + M2_CONV
        outs["p_hg"].append(s1p)
        outs["p_gdn"].append(s2p)
        outs["p_gc"].append(last3(pa, gq0, gq1))
        outs["p_ssm"].append(s3p)
        outs["p_sc"].append(last3(pb, mx0, mx1))
        outs["s_gc"].append(jnp.concatenate(
            [st_gc[l][:, 1:], lax.slice(pa, (n_prompt, gq0), (n_rows, gq1))[:, None]], axis=1))
        outs["s_sc"].append(jnp.concatenate(
            [st_sc[l][:, 1:], lax.slice(pb, (n_prompt, mx0), (n_rows, mx1))[:, None]], axis=1))
    y = _rmsnorm(x, w["final_norm"], f32, tr)
    o = {k: jnp.stack(v) for k, v in outs.items()}
    o.update(s_hg=s_hg, s_gdn=s_gdn, s_ssm=s_ssm)
    return y, o


def kernel(x_prompt, x_sample, state_hgrn, state_gdn, state_gdn_conv, state_ssm, state_ssm_conv, mix_norm, w_in, hg_lb, hg_onorm, gdn_conv, gdn_A_log, gdn_dt_bias, gdn_onorm, m2_conv_w, m2_conv_b, m2_dt_bias, m2_A_log, m2_D, m2_norm, w_branch, w_out, ffn_norm, w_ffn_in, w_ffn_out, final_norm):
    B, T, D = x_prompt.shape
    nb = x_sample.shape[0]
    w = {"mix_norm": mix_norm, "ffn_norm": ffn_norm, "final_norm": final_norm,
         "w_in": w_in, "w_branch": w_branch, "w_out": w_out,
         "w_ffn_in": w_ffn_in, "w_ffn_out": w_ffn_out.astype(bf16),
         "packed": _pack_params(hg_lb, hg_onorm, gdn_conv, gdn_A_log, gdn_dt_bias, gdn_onorm,
                                m2_conv_w, m2_conv_b, m2_dt_bias, m2_A_log, m2_D, m2_norm)}
    x_all = jnp.concatenate([x_prompt.reshape(B * T, D), x_sample.reshape(nb, D)], axis=0)
    states = (state_hgrn, state_gdn, state_gdn_conv, state_ssm, state_ssm_conv)
    y, o = _trunk(x_all, B * T, B, T, states, w, tm=1664, tm_ffn=832, tr=208, tb=512, nh=8)
    return (y[:B * T].reshape(B, T, D), y[B * T:].reshape(nb, 1, D),
            o["p_hg"], o["p_gdn"], o["p_gc"], o["p_ssm"], o["p_sc"],
            o["s_hg"], o["s_gdn"], o["s_gc"], o["s_ssm"], o["s_sc"])
```

```python
import functools

import jax
import jax.numpy as jnp
from jax import lax
from jax.experimental import pallas as pl
from jax.experimental.pallas import tpu as pltpu

f32 = jnp.float32
bf16 = jnp.bfloat16

D_MODEL = 4096
DEPTH = 4
HEADS = 8
HDIM = 128
HG_W = HEADS * HDIM
GDN_W = HEADS * HDIM
M2_INNER = D_MODEL // 2
M2_P = 64
M2_HEADS = M2_INNER // M2_P
M2_GROUPS = 4
M2_N = 128
M2_R = M2_HEADS // M2_GROUPS
M2_GW = M2_INNER // M2_GROUPS
M2_BC = M2_GROUPS * M2_N
M2_CONV = M2_INNER + 2 * M2_BC
CONV_W = 4
EPS = 1e-6
CHUNK = 64
SUB = 8
TRI = 16
LANES = 128
STEP_ROWS = 16
VMEM_LIMIT = 56 * 1024 * 1024

A_HQ, A_HF, A_HI, A_HG, A_GQKV = 0, 1024, 2048, 3072, 4096
B_GG, B_MZ, B_MXBC = 0, 1024, 3072
S_BETA, S_A, S_DT = 0, 8, 16
O_HG, O_GDN, O_M2 = 0, HG_W, HG_W + GDN_W


def _cparams(n_axes):
    return pltpu.CompilerParams(dimension_semantics=("arbitrary",) * n_axes,
                                vmem_limit_bytes=VMEM_LIMIT)


def _sigmoid(x):
    return 1.0 / (1.0 + jnp.exp(-x))


def _silu(x):
    return x * _sigmoid(x)


def _softplus(x):
    return jnp.maximum(x, 0.0) + jnp.log1p(jnp.exp(-jnp.abs(x)))


def _log_sigmoid(x):
    return jnp.minimum(x, 0.0) - jnp.log1p(jnp.exp(-jnp.abs(x)))


def _mxu(a, b, dims):
    return lax.dot_general(a, b, (dims, ((), ())), preferred_element_type=f32)


def _dot(a, b):
    return _mxu(a.astype(bf16), b.astype(bf16), ((1,), (0,)))


def _dot_nt(a, b):
    return _mxu(a.astype(bf16), b.astype(bf16), ((1,), (1,)))


def _dot_tn(a, b):
    return _mxu(a.astype(bf16), b.astype(bf16), ((0,), (0,)))


def _split2(x):
    x1 = x.astype(bf16)
    return x1, (x - x1.astype(f32)).astype(bf16)


def _split3(x):
    x1 = x.astype(bf16)
    r = x - x1.astype(f32)
    x2 = r.astype(bf16)
    return x1, x2, (r - x2.astype(f32)).astype(bf16)


def _dot2(a, b):
    a1, a2 = _split2(a)
    b1, b2 = _split2(b)
    d = lambda x, y: _mxu(x, y, ((1,), (0,)))
    return d(a1, b1) + (d(a1, b2) + d(a2, b1))


def _sel_dot(sel, x):
    return sum(_mxu(sel, t, ((1,), (0,))) for t in _split3(x))


def _dot_sel(x, sel):
    return sum(_mxu(t, sel, ((1,), (0,))) for t in _split3(x))


def _bcast_cols_tn(x_masked, ones):
    return sum(_mxu(t, ones, ((0,), (0,))) for t in _split3(x_masked))


def _tril_incl(n):
    r = lax.broadcasted_iota(jnp.int32, (n, n), 0)
    c = lax.broadcasted_iota(jnp.int32, (n, n), 1)
    return jnp.where(r >= c, 1.0, 0.0).astype(bf16)


def _rms(x, w):
    return x * lax.rsqrt(jnp.mean(x * x, axis=-1, keepdims=True) + EPS) * w


def _extract_col(x, lane_idx):
    lane = lax.broadcasted_iota(jnp.int32, x.shape, 1)
    return jnp.sum(jnp.where(lane == lane_idx, x, 0.0), axis=-1, keepdims=True)


def _row_mask(x, j):
    r = lax.broadcasted_iota(jnp.int32, x.shape, 0)
    return jnp.where(r == j, x, 0.0)


def _layer_spec(block, l, index_fn):
    return pl.BlockSpec((None,) + block, lambda *g: (l,) + index_fn(*g))


def _any_spec():
    return pl.BlockSpec(memory_space=pl.ANY)


def _rmsnorm_kernel(x_ref, w_ref, o_ref):
    o_ref[...] = _rms(x_ref[...], w_ref[...]).astype(o_ref.dtype)


def _rmsnorm(x, w, out_dtype, tr):
    m, d = x.shape
    return pl.pallas_call(
        _rmsnorm_kernel, grid=(m // tr,),
        in_specs=[pl.BlockSpec((tr, d), lambda i: (i, 0)), pl.BlockSpec((1, d), lambda i: (0, 0))],
        out_specs=pl.BlockSpec((tr, d), lambda i: (i, 0)),
        out_shape=jax.ShapeDtypeStruct((m, d), out_dtype),
        compiler_params=_cparams(1), name="rmsnorm")(x, w.reshape(1, d))


def _wdot(x, w):
    return jnp.dot(x, w.astype(bf16), preferred_element_type=f32)


def _wdot_nt(x, wt):
    return lax.dot_general(x, wt.astype(bf16), (((1,), (1,)), ((), ())), preferred_element_type=f32)


HI_ROWS = 64


def _mm_nt_kernel(x_ref, lo_ref, *rest, shift):
    o_ref = rest[-1]
    w = lo_ref[...]
    if shift:
        w = jnp.concatenate([w[shift:], rest[0][0:shift]], axis=0)
    o_ref[...] = _wdot_nt(x_ref[...], w)


def _mm_nt(x, wt, l, row0, n, tm, tn):
    m, k = x.shape
    base = (row0 // tn) * tn
    shift = row0 - base
    assert shift % 8 == 0 and shift <= HI_ROWS and tn % HI_ROWS == 0
    lo0 = base // tn
    hi0 = (base + tn) // HI_ROWS
    step = tn // HI_ROWS
    specs = [pl.BlockSpec((tm, k), lambda i, j: (i, 0)), _layer_spec((tn, k), l, lambda i, j: (lo0 + j, 0))]
    if shift:
        specs.append(_layer_spec((HI_ROWS, k), l, lambda i, j: (hi0 + step * j, 0)))
    return pl.pallas_call(
        functools.partial(_mm_nt_kernel, shift=shift), grid=(m // tm, n // tn), in_specs=specs,
        out_specs=pl.BlockSpec((tm, tn), lambda i, j: (i, j)),
        out_shape=jax.ShapeDtypeStruct((m, n), f32),
        compiler_params=_cparams(2), name="mm_nt")(*([x, wt] + ([wt] if shift else [])))


def _mm_small_kernel(x_ref, w1_ref, w2_ref, w3_ref, o_ref):
    k = w1_ref.shape[1]
    w = jnp.concatenate([w1_ref[...], w2_ref[...], w3_ref[...], jnp.zeros((LANES - 48, k), f32)], axis=0)
    o_ref[...] = _wdot_nt(x_ref[...], w)


def _mm_small(x, wt, l, tm):
    m, k = x.shape
    b1, b2 = 7168 // 16, 13328 // 16
    assert 7168 % 16 == 0 and 13328 % 16 == 0
    return pl.pallas_call(
        _mm_small_kernel, grid=(m // tm,),
        in_specs=[pl.BlockSpec((tm, k), lambda i: (i, 0)), _layer_spec((16, k), l, lambda i: (b1, 0)),
                  _layer_spec((16, k), l, lambda i: (b2, 0)), _layer_spec((16, k), l, lambda i: (b2 + 1, 0))],
        out_specs=pl.BlockSpec((tm, LANES), lambda i: (i, 0)),
        out_shape=jax.ShapeDtypeStruct((m, LANES), f32),
        compiler_params=_cparams(1), name="mm_small")(x, wt, wt, wt)


def _mm_resid_kernel(x_ref, w_ref, r_ref, o_ref):
    o_ref[...] = r_ref[...] + _wdot(x_ref[...], w_ref[...])


def _mm_resid(x, w, l, r, tm, tn):
    m, k = x.shape
    n = w.shape[2]
    return pl.pallas_call(
        _mm_resid_kernel, grid=(m // tm, n // tn),
        in_specs=[pl.BlockSpec((tm, k), lambda i, j: (i, 0)), _layer_spec((k, tn), l, lambda i, j: (0, j)),
                  pl.BlockSpec((tm, tn), lambda i, j: (i, j))],
        out_specs=pl.BlockSpec((tm, tn), lambda i, j: (i, j)),
        out_shape=jax.ShapeDtypeStruct((m, n), f32),
        compiler_params=_cparams(2), name="mm_resid")(x, w, r)


def _mm_swiglu_kernel(x_ref, wg_ref, wu_ref, o_ref):
    x = x_ref[...]
    g = _wdot(x, wg_ref[...])
    u = _wdot(x, wu_ref[...])
    o_ref[...] = (_silu(g) * u).astype(o_ref.dtype)


def _mm_swiglu(x, w, l, tm, tn):
    m, k = x.shape
    n = w.shape[2] // 2
    nb = n // tn
    return pl.pallas_call(
        _mm_swiglu_kernel, grid=(m // tm, nb),
        in_specs=[pl.BlockSpec((tm, k), lambda i, j: (i, 0)), _layer_spec((k, tn), l, lambda i, j: (0, j)),
                  _layer_spec((k, tn), l, lambda i, j: (0, j + nb))],
        out_specs=pl.BlockSpec((tm, tn), lambda i, j: (i, j)),
        out_shape=jax.ShapeDtypeStruct((m, n), bf16),
        compiler_params=_cparams(2), name="mm_swiglu")(x, w, w)


def _merge_kernel(o_ref, w_ref, ga_ref, gb_ref, gc_ref, out_ref):
    a = _wdot(o_ref[:, O_HG:O_GDN], w_ref[O_HG:O_GDN, :])
    b = _wdot(o_ref[:, O_GDN:O_M2], w_ref[O_GDN:O_M2, :])
    c = _wdot(o_ref[:, O_M2:], w_ref[O_M2:, :])
    out = _sigmoid(ga_ref[...]) * a + _sigmoid(gb_ref[...]) * b + _sigmoid(gc_ref[...]) * c
    out_ref[...] = out.astype(out_ref.dtype)


def _merge(o, wb, l, gates, tm, tn):
    m, k = o.shape
    n = wb.shape[2]
    gs = D_MODEL // tn
    return pl.pallas_call(
        _merge_kernel, grid=(m // tm, n // tn),
        in_specs=[pl.BlockSpec((tm, k), lambda i, j: (i, 0)), _layer_spec((k, tn), l, lambda i, j: (0, j)),
                  pl.BlockSpec((tm, tn), lambda i, j: (i, j)),
                  pl.BlockSpec((tm, tn), lambda i, j: (i, gs + j)),
                  pl.BlockSpec((tm, tn), lambda i, j: (i, 2 * gs + j))],
        out_specs=pl.BlockSpec((tm, tn), lambda i, j: (i, j)),
        out_shape=jax.ShapeDtypeStruct((m, n), bf16),
        compiler_params=_cparams(2), name="merge")(o, wb, gates, gates, gates)


def _alias_last_input(n_in, has_buf):
    return {n_in - 1: 0} if has_buf else {}


def _hgrn_inputs(qz, z, par):
    q = _silu(qz)
    a = par[0:1]
    b = par[1:2] + _log_sigmoid(z)
    m = jnp.maximum(a, b)
    logf = m + jnp.log(jnp.exp(a - m) + jnp.exp(b - m))
    k = par[2:3] / (1.0 + jnp.exp(z))
    return q, k, logf


def _hgrn_chunk(q, k, logf, v, st, tril, lane_c, sub_r):
    C = CHUNK
    G = _sel_dot(tril, logf)
    o = _dot_nt(q * jnp.exp(G), st)
    a_rows = []
    for a in range(C // SUB):
        lo = a * SUB
        Ga, qa, ka = G[lo:lo + SUB], q[lo:lo + SUB], k[lo:lo + SUB]
        if a > 0:
            Gs = G[lo - 1:lo]
            qt = qa * jnp.exp(Ga - Gs)
            kt = k * jnp.exp(jnp.minimum(Gs - G, 0.0))
            R = jnp.where(lane_c < lo, _dot_nt(qt, kt), 0.0)
        else:
            R = jnp.zeros((SUB, C), f32)
        for jl in range(SUB):
            e = qa * ka[jl:jl + 1] * jnp.exp(jnp.minimum(Ga - Ga[jl:jl + 1], 0.0))
            c = jnp.sum(e, axis=-1, keepdims=True)
            R = jnp.where(lane_c == lo + jl, jnp.where(sub_r >= jl, c, 0.0), R)
        a_rows.append(R)
    A = jnp.concatenate(a_rows, axis=0)
    o = o + _dot(A, v)
    Gl = G[C - 1:C]
    return o, st * jnp.exp(Gl) + _dot_tn(v, k * jnp.exp(Gl - G))


def _hgrn_prompt_kernel(q_ref, f_ref, i_ref, g_ref, par_ref, *rest, nchunk, nh, has_buf):
    o_ref, s_ref, st_scr = rest[1:] if has_buf else rest
    t = pl.program_id(2)
    C = CHUNK

    @pl.when(t == 0)
    def _():
        st_scr[...] = jnp.zeros_like(st_scr)

    tril = _tril_incl(C)
    lane_c = lax.broadcasted_iota(jnp.int32, (SUB, C), 1)
    sub_r = lax.broadcasted_iota(jnp.int32, (SUB, C), 0)

    def chunk(ci, carry):
        rows = pl.ds(pl.multiple_of(ci * C, C), C)
        for j in range(nh):
            cols = slice(j * LANES, (j + 1) * LANES)
            par = par_ref[:, cols]
            q, k, logf = _hgrn_inputs(q_ref[rows, cols], f_ref[rows, cols], par)
            o, st = _hgrn_chunk(q, k, logf, i_ref[rows, cols], st_scr[j], tril, lane_c, sub_r)
            st_scr[j] = st
            o_ref[rows, cols] = (_rms(o, par[3:4]) * _silu(g_ref[rows, cols])).astype(o_ref.dtype)
        return carry

    lax.fori_loop(0, nchunk, chunk, 0)

    @pl.when(t == pl.num_programs(2) - 1)
    def _():
        for j in range(nh):
            s_ref[0, j] = st_scr[j].T


def _hgrn_prompt(pa, par, l, obuf, n_rows, B, T, tb, nh):
    nt = T // tb
    w = nh * LANES
    cb = lambda off: (lambda b, h, t: (b * nt + t, off // w + h))
    ins = [pa, pa, pa, pa, par] + ([obuf] if obuf is not None else [])
    in_specs = [pl.BlockSpec((tb, w), cb(A_HQ)), pl.BlockSpec((tb, w), cb(A_HF)),
                pl.BlockSpec((tb, w), cb(A_HI)), pl.BlockSpec((tb, w), cb(A_HG)),
                _layer_spec((8, w), l, lambda b, h, t: (0, h))] + ([_any_spec()] if obuf is not None else [])
    return pl.pallas_call(
        functools.partial(_hgrn_prompt_kernel, nchunk=tb // CHUNK, nh=nh, has_buf=obuf is not None),
        grid=(B, HEADS // nh, nt), in_specs=in_specs,
        out_specs=[pl.BlockSpec((tb, w), cb(O_HG)),
                   pl.BlockSpec((1, nh, HDIM, HDIM), lambda b, h, t: (b, h, 0, 0))],
        out_shape=[jax.ShapeDtypeStruct((n_rows, D_MODEL), bf16),
                   jax.ShapeDtypeStruct((B, HEADS, HDIM, HDIM), f32)],
        scratch_shapes=[pltpu.VMEM((nh, HDIM, HDIM), f32)],
        input_output_aliases=_alias_last_input(len(ins), obuf is not None),
        compiler_params=_cparams(3), name="hgrn_prompt")(*ins)


def _hgrn_step_kernel(q_ref, f_ref, i_ref, g_ref, par_ref, s_ref, *rest):
    o_ref, so_ref = rest[-2:]
    R = STEP_ROWS
    par = par_ref[...]
    q, k, logf = _hgrn_inputs(q_ref[...], f_ref[...], par)
    v = i_ref[...]
    ef = jnp.exp(logf)
    ones = jnp.ones((R, LANES), bf16)
    row = lax.broadcasted_iota(jnp.int32, (R, LANES), 0)
    o = jnp.zeros((R, LANES), f32)
    for j in range(R):
        so_ref[j, 0] = s_ref[j, 0] * _bcast_cols_tn(_row_mask(ef, j), ones) + _dot_tn(_row_mask(k, j), v)
    for j in range(R):
        o = jnp.where(row == j, _dot(q, so_ref[j, 0]), o)
    o_ref[...] = (_rms(o, par[3:4]) * _silu(g_ref[...])).astype(o_ref.dtype)


def _hgrn_step(pa, par, l, states, obuf, sbuf, row0, nb):
    R = STEP_ROWS
    rb = row0 // R
    cb = lambda off: (lambda i, h: (rb + i, off // LANES + h))
    st_spec = _layer_spec((R, 1, HDIM, HDIM), l, lambda i, h: (i, h, 0, 0))
    ins = [pa, pa, pa, pa, par, states, obuf] + ([sbuf] if sbuf is not None else [])
    aliases = {6: 0, 7: 1} if sbuf is not None else {6: 0}
    return pl.pallas_call(
        _hgrn_step_kernel, grid=(nb // R, HEADS),
        in_specs=[pl.BlockSpec((R, LANES), cb(A_HQ)), pl.BlockSpec((R, LANES), cb(A_HF)),
                  pl.BlockSpec((R, LANES), cb(A_HI)), pl.BlockSpec((R, LANES), cb(A_HG)),
                  _layer_spec((8, LANES), l, lambda i, h: (0, h)), st_spec, _any_spec()]
        + ([_any_spec()] if sbuf is not None else []),
        out_specs=[pl.BlockSpec((R, LANES), cb(O_HG)), st_spec],
        out_shape=[jax.ShapeDtypeStruct(obuf.shape, bf16), jax.ShapeDtypeStruct(states.shape, f32)],
        input_output_aliases=aliases,
        compiler_params=_cparams(2), name="hgrn_step")(*ins)


def _l2norm(t):
    return t * lax.rsqrt(jnp.sum(t * t, axis=-1, keepdims=True) + EPS)


def _gdn_gates(sm, p2, h):
    beta = _extract_col(_sigmoid(sm), S_BETA + h)
    g_all = -jnp.exp(p2[1:2]) * _softplus(sm + p2[0:1])
    return beta, _extract_col(g_all, S_A + h)


def _conv_block(xp_scr, x_ref, w, tb, first):
    @pl.when(first)
    def _():
        xp_scr[0:8, :] = jnp.zeros((8, xp_scr.shape[1]), f32)

    xp_scr[8:8 + tb, :] = x_ref[...]
    out = xp_scr[5:5 + tb, :] * w[0:1]
    for j in range(1, CONV_W):
        out = out + xp_scr[5 + j:5 + j + tb, :] * w[j:j + 1]
    xp_scr[5:8, :] = xp_scr[tb + 5:tb + 8, :]
    return out


def _solve_unit_lower(N, rhs, r, c):
    eye = jnp.where(r == c, 1.0, 0.0)
    blk = (r // TRI) == (c // TRI)
    Nd = jnp.where(blk, N, 0.0)
    No = jnp.where(blk, 0.0, N)
    D = eye + Nd
    P = Nd
    for _ in range(3):
        P = _dot2(P, P)
        yield
        D = D + _dot2(D, P)
        yield
    M = _dot2(D, No)
    Y = _dot2(D, rhs)
    yield
    M2 = _dot2(M, M)
    yield
    W = eye + M
    W = W + _dot2(W, M2)
    yield
    return _dot2(W, Y)


def _gdn_chunk(q, k, v, beta, g, S, tril, r, c):
    C = CHUNK
    gb = jnp.broadcast_to(g, (C, LANES))
    Gb = _sel_dot(tril, gb)
    Dm = _sel_dot(tril, jnp.where(r > c, gb[:, 0:C], 0.0))
    kb = k * beta
    kk = _dot_nt(kb, k)
    qk = _dot_nt(q, k)
    yield
    dec = jnp.where(r >= c, jnp.exp(jnp.minimum(Dm, 0.0)), 0.0)
    eG = jnp.exp(Gb)
    N = jnp.where(r > c, -kk * dec, 0.0)
    X = yield from _solve_unit_lower(N, jnp.concatenate([v * beta, kb * eG], axis=1), r, c)
    yield
    Vn = X[:, 0:HDIM] - _dot(X[:, HDIM:], S)
    o_state = _dot(q * eG, S)
    yield
    o = o_state + _dot(qk * dec, Vn)
    Gl = Gb[C - 1:C]
    return o, S * jnp.exp(Gl) + _dot_tn(k * jnp.exp(Gl - Gb), Vn)


def _round_robin(gens):
    results = [None] * len(gens)
    live = list(range(len(gens)))
    while live:
        for i in list(live):
            try:
                next(gens[i])
            except StopIteration as stop:
                results[i] = stop.value
                live.remove(i)
    return results


def _gdn_prompt_kernel(q_ref, k_ref, v_ref, g_ref, sm_ref, wq_ref, wk_ref, wv_ref, p2_ref, obuf_ref,
                       o_ref, s_ref, xq, xk, xv, cq, ck, cv, s_scr, *, nchunk, tb, nh):
    hb = pl.program_id(1)
    t = pl.program_id(2)
    C = CHUNK
    first = t == 0

    @pl.when(first)
    def _():
        s_scr[...] = jnp.zeros_like(s_scr)

    cq[...] = _silu(_conv_block(xq, q_ref, wq_ref[...], tb, first))
    ck[...] = _silu(_conv_block(xk, k_ref, wk_ref[...], tb, first))
    cv[...] = _silu(_conv_block(xv, v_ref, wv_ref[...], tb, first))

    p2 = p2_ref[...]
    tril = _tril_incl(C)
    r = lax.broadcasted_iota(jnp.int32, (C, C), 0)
    c = lax.broadcasted_iota(jnp.int32, (C, C), 1)

    def chunk(ci, carry):
        rows = pl.ds(pl.multiple_of(ci * C, C), C)
        sm = sm_ref[rows, :]
        gens = []
        for j in range(nh):
            cols = slice(j * LANES, (j + 1) * LANES)
            q = _l2norm(cq[rows, cols]) * (HDIM ** -0.5)
            k = _l2norm(ck[rows, cols])
            beta, g = _gdn_gates(sm, p2, hb * nh + j)
            gens.append(_gdn_chunk(q, k, cv[rows, cols], beta, g, s_scr[j], tril, r, c))
        for j, (o, S) in enumerate(_round_robin(gens)):
            cols = slice(j * LANES, (j + 1) * LANES)
            s_scr[j] = S
            o_ref[rows, cols] = (_rms(o, p2[2:3]) * _silu(g_ref[rows, cols])).astype(o_ref.dtype)
        return carry

    lax.fori_loop(0, nchunk, chunk, 0)

    @pl.when(t == pl.num_programs(2) - 1)
    def _():
        s_ref[0] = s_scr[...]


def _gdn_prompt(pa, pb, ps, pconv, p2, l, obuf, B, T, tb, nh):
    nt = T // tb
    w = nh * LANES
    cb = lambda off: (lambda b, h, t: (b * nt + t, off // w + h))
    wb = lambda off: (lambda b, h, t: (0, off // w + h))
    return pl.pallas_call(
        functools.partial(_gdn_prompt_kernel, nchunk=tb // CHUNK, tb=tb, nh=nh),
        grid=(B, HEADS // nh, nt),
        in_specs=[pl.BlockSpec((tb, w), cb(A_GQKV)), pl.BlockSpec((tb, w), cb(A_GQKV + GDN_W)),
                  pl.BlockSpec((tb, w), cb(A_GQKV + 2 * GDN_W)), pl.BlockSpec((tb, w), cb(B_GG)),
                  pl.BlockSpec((tb, LANES), lambda b, h, t: (b * nt + t, 0)),
                  _layer_spec((8, w), l, wb(0)), _layer_spec((8, w), l, wb(GDN_W)),
                  _layer_spec((8, w), l, wb(2 * GDN_W)),
                  _layer_spec((8, LANES), l, lambda b, h, t: (0, 0)), _any_spec()],
        out_specs=[pl.BlockSpec((tb, w), cb(O_GDN)),
                   pl.BlockSpec((1, nh, HDIM, HDIM), lambda b, h, t: (b, h, 0, 0))],
        out_shape=[jax.ShapeDtypeStruct(obuf.shape, bf16),
                   jax.ShapeDtypeStruct((B, HEADS, HDIM, HDIM), f32)],
        scratch_shapes=[pltpu.VMEM((tb + 8, w), f32)] * 3 + [pltpu.VMEM((tb, w), f32)] * 3
        + [pltpu.VMEM((nh, HDIM, HDIM), f32)],
        input_output_aliases={9: 0},
        compiler_params=_cparams(3), name="gdn_prompt")(pa, pa, pa, pb, ps, pconv, pconv, pconv, p2, obuf)


def _conv_step(x, b0, b1, b2, w):
    return b0 * w[0:1] + b1 * w[1:2] + b2 * w[2:3] + x * w[3:4]


def _gdn_step_kernel(q_ref, k_ref, v_ref, g_ref, sm_ref, q0, q1, q2, k0, k1, k2, v0, v1, v2,
                     wq_ref, wk_ref, wv_ref, p2_ref, s_ref, *rest):
    o_ref, so_ref = rest[-2:]
    R = STEP_ROWS
    h = pl.program_id(1)
    p2 = p2_ref[...]
    q = _l2norm(_silu(_conv_step(q_ref[...], q0[...], q1[...], q2[...], wq_ref[...]))) * (HDIM ** -0.5)
    k = _l2norm(_silu(_conv_step(k_ref[...], k0[...], k1[...], k2[...], wk_ref[...])))
    v = _silu(_conv_step(v_ref[...], v0[...], v1[...], v2[...], wv_ref[...]))
    beta, g = _gdn_gates(sm_ref[...], p2, h)
    eg = jnp.exp(jnp.broadcast_to(g, (R, LANES)))
    row = lax.broadcasted_iota(jnp.int32, (R, LANES), 0)
    qs = jnp.zeros((R, LANES), f32)
    ks = jnp.zeros((R, LANES), f32)
    for j in range(R):
        s = s_ref[j, 0]
        qs = jnp.where(row == j, _dot(q, s), qs)
        ks = jnp.where(row == j, _dot(k, s), ks)
    vn = beta * v - (beta * eg) * ks
    o = eg * qs + jnp.sum(q * k, axis=-1, keepdims=True) * vn
    for j in range(R):
        so_ref[j, 0] = (s_ref[j, 0] * jnp.broadcast_to(eg[j:j + 1, 0:1], (HDIM, HDIM))
                        + _dot_tn(_row_mask(k, j), vn))
    o_ref[...] = (_rms(o, p2[2:3]) * _silu(g_ref[...])).astype(o_ref.dtype)


def _gdn_step(pa, pb, ps, conv_t, pconv, p2, l, states, obuf, sbuf, row0, nb):
    R = STEP_ROWS
    rb = row0 // R
    cb = lambda off: (lambda i, h: (rb + i, off // LANES + h))
    wb = lambda off: (lambda i, h: (0, off // LANES + h))
    tap = lambda off, j: pl.BlockSpec((None, None, R, LANES), lambda i, h: (l, j, i, off // LANES + h))
    taps = [tap(off, j) for off in (0, GDN_W, 2 * GDN_W) for j in range(CONV_W - 1)]
    st_spec = _layer_spec((R, 1, HDIM, HDIM), l, lambda i, h: (i, h, 0, 0))
    ins = [pa, pa, pa, pb, ps] + [conv_t] * 9 + [pconv, pconv, pconv, p2, states, obuf] + (
        [sbuf] if sbuf is not None else [])
    n = len(ins)
    aliases = {n - 2: 0, n - 1: 1} if sbuf is not None else {n - 1: 0}
    return pl.pallas_call(
        _gdn_step_kernel, grid=(nb // R, HEADS),
        in_specs=[pl.BlockSpec((R, LANES), cb(A_GQKV)), pl.BlockSpec((R, LANES), cb(A_GQKV + GDN_W)),
                  pl.BlockSpec((R, LANES), cb(A_GQKV + 2 * GDN_W)), pl.BlockSpec((R, LANES), cb(B_GG)),
                  pl.BlockSpec((R, LANES), lambda i, h: (rb + i, 0))] + taps
        + [_layer_spec((8, LANES), l, wb(0)), _layer_spec((8, LANES), l, wb(GDN_W)),
           _layer_spec((8, LANES), l, wb(2 * GDN_W)), _layer_spec((8, LANES), l, lambda i, h: (0, 0)),
           st_spec, _any_spec()] + ([_any_spec()] if sbuf is not None else []),
        out_specs=[pl.BlockSpec((R, LANES), cb(O_GDN)), st_spec],
        out_shape=[jax.ShapeDtypeStruct(obuf.shape, bf16), jax.ShapeDtypeStruct(states.shape, f32)],
        input_output_aliases=aliases,
        compiler_params=_cparams(2), name="gdn_step")(*ins)


def _head_select(g, width):
    lane = lax.broadcasted_iota(jnp.int32, (LANES, M2_R * width), 0)
    col = lax.broadcasted_iota(jnp.int32, (LANES, M2_R * width), 1)
    return jnp.where(lane == S_DT + g * M2_R + col // width, 1.0, 0.0).astype(bf16)


def _ssd_chunk(x, Bm, Cm, sm, px, sel, st, y_view, tril, r, cmod):
    C = CHUNK
    P = M2_P
    dt_raw = _dot_sel(sm, sel)
    CB = _dot_nt(Cm, Bm)
    y_state = _dot(Cm, st)
    yield
    dt = _softplus(dt_raw + px[5:6])
    a = dt * -jnp.exp(px[6:7])
    GW = _sel_dot(tril, a)
    Dm = _sel_dot(tril, jnp.where(r > cmod, a, 0.0))
    yield
    dec = jnp.where(r >= cmod, jnp.exp(jnp.minimum(Dm, 0.0)), 0.0)
    xdt = x * dt
    for h in range(M2_R):
        sl = slice(h * P, (h + 1) * P)
        y_view[:, sl] = _dot(CB * dec[:, sl], xdt[:, sl])
    Gl = GW[C - 1:C]
    st_new = st * jnp.exp(Gl) + _dot_tn(Bm, xdt * jnp.exp(Gl - GW))
    yield
    return y_view[...] + y_state * jnp.exp(GW) + x * px[7:8], st_new


def _ssd_prompt_kernel(z_ref, x_ref, b_ref, c_ref, sm_ref, px_ref, pb_ref, pc_ref, obuf_ref,
                       o_ref, s_ref, xx, xb, xc, cx, cb_s, cc, y_scr, st_scr, *, nchunk, tb, ng):
    gb = pl.program_id(1)
    t = pl.program_id(2)
    C = CHUNK
    first = t == 0

    @pl.when(first)
    def _():
        st_scr[...] = jnp.zeros_like(st_scr)

    cx[...] = _silu(_conv_block(xx, x_ref, px_ref[...], tb, first) + px_ref[4:5, :])
    cb_s[...] = _silu(_conv_block(xb, b_ref, pb_ref[...], tb, first) + pb_ref[4:5, :])
    cc[...] = _silu(_conv_block(xc, c_ref, pc_ref[...], tb, first) + pc_ref[4:5, :])

    tril = _tril_incl(C)
    r = lax.broadcasted_iota(jnp.int32, (C, M2_GW), 0)
    cmod = lax.broadcasted_iota(jnp.int32, (C, M2_GW), 1) % M2_P
    sels = [_head_select(gb * ng + i, M2_P) for i in range(ng)]

    def chunk(ci, carry):
        rows = pl.ds(pl.multiple_of(ci * C, C), C)
        sm = sm_ref[rows, :]
        gens = []
        for i in range(ng):
            xc_ = slice(i * M2_GW, (i + 1) * M2_GW)
            nc = slice(i * M2_N, (i + 1) * M2_N)
            gens.append(_ssd_chunk(cx[rows, xc_], cb_s[rows, nc], cc[rows, nc], sm, px_ref[:, xc_], sels[i],
                                   st_scr[i], y_scr.at[:, xc_], tril, r, cmod))
        for i, (y, st) in enumerate(_round_robin(gens)):
            xc_ = slice(i * M2_GW, (i + 1) * M2_GW)
            st_scr[i] = st
            y = y * _silu(z_ref[rows, xc_])
            o_ref[rows, xc_] = _rms(y, px_ref[8:9, xc_]).astype(o_ref.dtype)
        return carry

    lax.fori_loop(0, nchunk, chunk, 0)

    @pl.when(t == pl.num_programs(2) - 1)
    def _():
        for i in range(ng):
            s_ref[0, i * M2_R:(i + 1) * M2_R] = st_scr[i].T.reshape(M2_R, M2_P, M2_N)


def _ssd_prompt(pb, ps, px, pbc, l, obuf, B, T, tb, ng):
    nt = T // tb
    wx = ng * M2_GW
    wn = ng * M2_N
    xs = lambda off: (lambda b, g, t: (b * nt + t, off // wx + g))
    bc = lambda off: (lambda b, g, t: (b * nt + t, off // wn + g))
    return pl.pallas_call(
        functools.partial(_ssd_prompt_kernel, nchunk=tb // CHUNK, tb=tb, ng=ng),
        grid=(B, M2_GROUPS // ng, nt),
        in_specs=[pl.BlockSpec((tb, wx), xs(B_MZ)), pl.BlockSpec((tb, wx), xs(B_MXBC)),
                  pl.BlockSpec((tb, wn), bc(B_MXBC + M2_INNER)),
                  pl.BlockSpec((tb, wn), bc(B_MXBC + M2_INNER + M2_BC)),
                  pl.BlockSpec((tb, LANES), lambda b, g, t: (b * nt + t, 0)),
                  _layer_spec((16, wx), l, lambda b, g, t: (0, g)),
                  _layer_spec((8, wn), l, lambda b, g, t: (0, g)),
                  _layer_spec((8, wn), l, lambda b, g, t: (0, M2_BC // wn + g)), _any_spec()],
        out_specs=[pl.BlockSpec((tb, wx), xs(O_M2)),
                   pl.BlockSpec((1, ng * M2_R, M2_P, M2_N), lambda b, g, t: (b, g, 0, 0))],
        out_shape=[jax.ShapeDtypeStruct(obuf.shape, bf16),
                   jax.ShapeDtypeStruct((B, M2_HEADS, M2_P, M2_N), f32)],
        scratch_shapes=[pltpu.VMEM((tb + 8, wx), f32), pltpu.VMEM((tb + 8, wn), f32),
                        pltpu.VMEM((tb + 8, wn), f32), pltpu.VMEM((tb, wx), f32),
                        pltpu.VMEM((tb, wn), f32), pltpu.VMEM((tb, wn), f32),
                        pltpu.VMEM((CHUNK, wx), f32), pltpu.VMEM((ng, M2_N, M2_GW), f32)],
        input_output_aliases={8: 0},
        compiler_params=_cparams(3), name="ssd_prompt")(pb, pb, pb, pb, ps, px, pbc, pbc, obuf)


def _ssd_step_kernel(z_ref, x_ref, b_ref, c_ref, sm_ref, x0, x1, x2, b0, b1, b2, c0, c1, c2,
                     px_ref, pb_ref, pc_ref, s_ref, *rest):
    o_ref, so_ref, y_scr = rest[-3:]
    R = STEP_ROWS
    g = pl.program_id(1)
    P = M2_P
    px = px_ref[...]
    pb = pb_ref[...]
    pc = pc_ref[...]
    x = _silu(_conv_step(x_ref[...], x0[...], x1[...], x2[...], px) + px[4:5])
    Bm = _silu(_conv_step(b_ref[...], b0[...], b1[...], b2[...], pb) + pb[4:5])
    Cm = _silu(_conv_step(c_ref[...], c0[...], c1[...], c2[...], pc) + pc[4:5])
    dt = _softplus(_dot_sel(sm_ref[...], _head_select(g, P)) + px[5:6])
    ea = jnp.exp(dt * -jnp.exp(px[6:7]))
    xdt = x * dt
    row = lax.broadcasted_iota(jnp.int32, (R, P), 0)
    for h in range(M2_R):
        sl = slice(h * P, (h + 1) * P)
        yh = jnp.zeros((R, P), f32)
        for j in range(R):
            s = s_ref[j, h]
            yh = jnp.where(row == j, _dot_nt(Cm, s), yh)
            scale = jnp.broadcast_to(ea[j:j + 1, h * P:h * P + 1], (P, M2_N))
            so_ref[j, h] = s * scale + _dot_tn(_row_mask(xdt[:, sl], j), Bm)
        y_scr[:, sl] = yh
    y = y_scr[...] * ea + jnp.sum(Cm * Bm, axis=-1, keepdims=True) * xdt + x * px[7:8]
    y = y * _silu(z_ref[...])
    o_ref[...] = _rms(y, px[8:9]).astype(o_ref.dtype)


def _ssd_step(pb, ps, conv_t, px, pbc, l, states, obuf, sbuf, row0, nb):
    R = STEP_ROWS
    rb = row0 // R
    xs = lambda off: (lambda i, g: (rb + i, off // M2_GW + g))
    bc = lambda off: (lambda i, g: (rb + i, off // LANES + g))
    xtap = lambda j: pl.BlockSpec((None, None, R, M2_GW), lambda i, g: (l, j, i, g))
    btap = lambda off, j: pl.BlockSpec((None, None, R, LANES), lambda i, g: (l, j, i, off // LANES + g))
    taps = ([xtap(j) for j in range(3)] + [btap(M2_INNER, j) for j in range(3)]
            + [btap(M2_INNER + M2_BC, j) for j in range(3)])
    st_spec = _layer_spec((R, M2_R, M2_P, M2_N), l, lambda i, g: (i, g, 0, 0))
    ins = [pb, pb, pb, pb, ps] + [conv_t] * 9 + [px, pbc, pbc, states, obuf] + ([sbuf] if sbuf is not None else [])
    n = len(ins)
    aliases = {n - 2: 0, n - 1: 1} if sbuf is not None else {n - 1: 0}
    return pl.pallas_call(
        _ssd_step_kernel, grid=(nb // R, M2_GROUPS),
        in_specs=[pl.BlockSpec((R, M2_GW), xs(B_MZ)), pl.BlockSpec((R, M2_GW), xs(B_MXBC)),
                  pl.BlockSpec((R, LANES), bc(B_MXBC + M2_INNER)),
                  pl.BlockSpec((R, LANES), bc(B_MXBC + M2_INNER + M2_BC)),
                  pl.BlockSpec((R, LANES), lambda i, g: (rb + i, 0))] + taps
        + [_layer_spec((16, M2_GW), l, lambda i, g: (0, g)),
           _layer_spec((8, LANES), l, lambda i, g: (0, g)),
           _layer_spec((8, LANES), l, lambda i, g: (0, M2_GROUPS + g)),
           st_spec, _any_spec()] + ([_any_spec()] if sbuf is not None else []),
        out_specs=[pl.BlockSpec((R, M2_GW), xs(O_M2)), st_spec],
        out_shape=[jax.ShapeDtypeStruct(obuf.shape, bf16), jax.ShapeDtypeStruct(states.shape, f32)],
        scratch_shapes=[pltpu.VMEM((R, M2_GW), f32)],
        input_output_aliases=aliases,
        compiler_params=_cparams(2), name="ssd_step")(*ins)


def _pad_rows(a, rows):
    return jnp.pad(a, ((0, 0), (0, rows - a.shape[1]), (0, 0)))


def _pack_params(hg_lb, hg_onorm, gdn_conv, gdn_A_log, gdn_dt_bias, gdn_onorm,
                 m2_conv_w, m2_conv_b, m2_dt_bias, m2_A_log, m2_D, m2_norm):
    lb = jnp.cumsum(jax.nn.softmax(hg_lb.astype(f32), axis=0), axis=0)
    lb = lb - lb[0]
    hg_par = _pad_rows(jnp.stack([jnp.log(lb), jnp.log1p(-lb), 1.0 - lb,
                                  jnp.tile(hg_onorm, (1, HEADS))], axis=1), 8)
    gdn_pconv = _pad_rows(gdn_conv, 8)
    lane_pad = lambda a: jnp.pad(a, ((0, 0), (S_A, LANES - S_A - HEADS)))
    gdn_p2 = _pad_rows(jnp.stack([lane_pad(gdn_dt_bias), lane_pad(gdn_A_log), gdn_onorm], axis=1), 8)
    rep = lambda a: jnp.repeat(a, M2_P, axis=1)
    ssd_px = _pad_rows(jnp.concatenate(
        [m2_conv_w[:, :, :M2_INNER],
         jnp.stack([m2_conv_b[:, :M2_INNER], rep(m2_dt_bias), rep(m2_A_log), rep(m2_D), m2_norm], axis=1)],
        axis=1), 16)
    ssd_pbc = _pad_rows(jnp.concatenate([m2_conv_w[:, :, M2_INNER:], m2_conv_b[:, None, M2_INNER:]], axis=1), 8)
    return hg_par, gdn_pconv, gdn_p2, ssd_px, ssd_pbc


def _trunk(x_all, n_prompt, B, T, states, w, *, tm, tm_ffn, tr, tb, nh):
    st_hg, st_gdn, st_gc, st_ssm, st_sc = states
    n_rows = x_all.shape[0]
    nb = n_rows - n_prompt
    hg_par, gdn_pconv, gdn_p2, ssd_px, ssd_pbc = w["packed"]
    w_in_t = jnp.swapaxes(w["w_in"], 1, 2)
    gc_t = jnp.swapaxes(st_gc, 1, 2)
    sc_t = jnp.swapaxes(st_sc, 1, 2)
    outs = {k: [] for k in ("p_hg", "p_gdn", "p_gc", "p_ssm", "p_sc", "s_gc", "s_sc")}
    s_hg = s_gdn = s_ssm = None
    x = x_all
    last3 = lambda p, c0, c1: jnp.stack([lax.slice(p, (b * T + T - (CONV_W - 1), c0), (b * T + T, c1))
                                         for b in range(B)])
    for l in range(DEPTH):
        h = _rmsnorm(x, w["mix_norm"][l], bf16, tr)
        pa = _mm_nt(h, w_in_t, l, 0, 7168, tm, 256)
        pb = _mm_nt(h, w_in_t, l, 7184, 6144, tm, 256)
        pc = _mm_nt(h, w_in_t, l, 13360, 3 * D_MODEL, tm, 256)
        ps = _mm_small(h, w_in_t, l, tm)
        o, s1p = _hgrn_prompt(pa, hg_par, l, None, n_rows, B, T, tb, nh)
        o, s2p = _gdn_prompt(pa, pb, ps, gdn_pconv, gdn_p2, l, o, B, T, tb, nh)
        o, s3p = _ssd_prompt(pb, ps, ssd_px, ssd_pbc, l, o, B, T, tb // 2, 2)
        o, s_hg = _hgrn_step(pa, hg_par, l, st_hg, o, s_hg, n_prompt, nb)
        o, s_gdn = _gdn_step(pa, pb, ps, gc_t, gdn_pconv, gdn_p2, l, st_gdn, o, s_gdn, n_prompt, nb)
        o, s_ssm = _ssd_step(pb, ps, sc_t, ssd_px, ssd_pbc, l, st_ssm, o, s_ssm, n_prompt, nb)
        merged = _merge(o, w["w_branch"], l, pc, tm, 256)
        x = _mm_resid(merged, w["w_out"], l, x, tm, 256)
        h2 = _rmsnorm(x, w["ffn_norm"][l], bf16, tr)
        act = _mm_swiglu(h2, w["w_ffn_in"], l, tm, 256)
        x = _mm_resid(act, w["w_ffn_out"], l, x, tm_ffn, 256)
        gq0, gq1 = A_GQKV, A_GQKV + 3 * GDN_W
        mx0, mx1 = B_MXBC, B_MXBC + M2_CONV
        outs["p_hg"].append(s1p)
        outs["p_gdn"].append(s2p)
        outs["p_gc"].append(last3(pa, gq0, gq1))
        outs["p_ssm"].append(s3p)
        outs["p_sc"].append(last3(pb, mx0, mx1))
        outs["s_gc"].append(jnp.concatenate(
            [st_gc[l][:, 1:], lax.slice(pa, (n_prompt, gq0), (n_rows, gq1))[:, None]], axis=1))
        outs["s_sc"].append(jnp.concatenate(
            [st_sc[l][:, 1:], lax.slice(pb, (n_prompt, mx0), (n_rows, mx1))[:, None]], axis=1))
    y = _rmsnorm(x, w["final_norm"], f32, tr)
    o = {k: jnp.stack(v) for k, v in outs.items()}
    o.update(s_hg=s_hg, s_gdn=s_gdn, s_ssm=s_ssm)
    return y, o


def kernel(x_prompt, x_sample, state_hgrn, state_gdn, state_gdn_conv, state_ssm, state_ssm_conv, mix_norm, w_in, hg_lb, hg_onorm, gdn_conv, gdn_A_log, gdn_dt_bias, gdn_onorm, m2_conv_w, m2_conv_b, m2_dt_bias, m2_A_log, m2_D, m2_norm, w_branch, w_out, ffn_norm, w_ffn_in, w_ffn_out, final_norm):
    B, T, D = x_prompt.shape
    nb = x_sample.shape[0]
    w = {"mix_norm": mix_norm, "ffn_norm": ffn_norm, "final_norm": final_norm,
         "w_in": w_in, "w_branch": w_branch, "w_out": w_out,
         "w_ffn_in": w_ffn_in, "w_ffn_out": w_ffn_out.astype(bf16),
         "packed": _pack_params(hg_lb, hg_onorm, gdn_conv, gdn_A_log, gdn_dt_bias, gdn_onorm,
                                m2_conv_w, m2_conv_b, m2_dt_bias, m2_A_log, m2_D, m2_norm)}
    x_all = jnp.concatenate([x_prompt.reshape(B * T, D), x_sample.reshape(nb, D)], axis=0)
    states = (state_hgrn, state_gdn, state_gdn_conv, state_ssm, state_ssm_conv)
    y, o = _trunk(x_all, B * T, B, T, states, w, tm=1664, tm_ffn=832, tr=208, tb=512, nh=8)
    return (y[:B * T].reshape(B, T, D), y[B * T:].reshape(nb, 1, D),
            o["p_hg"], o["p_gdn"], o["p_gc"], o["p_ssm"], o["p_sc"],
            o["s_hg"], o["s_gdn"], o["s_gc"], o["s_ssm"], o["s_sc"])
```

```python
import functools

import jax
import jax.numpy as jnp
from jax import lax
from jax.experimental import pallas as pl
from jax.experimental.pallas import tpu as pltpu

f32 = jnp.float32
bf16 = jnp.bfloat16

D_MODEL = 4096
DEPTH = 4
HEADS = 8
HDIM = 128
HG_W = HEADS * HDIM
GDN_W = HEADS * HDIM
M2_INNER = D_MODEL // 2
M2_P = 64
M2_HEADS = M2_INNER // M2_P
M2_GROUPS = 4
M2_N = 128
M2_R = M2_HEADS // M2_GROUPS
M2_GW = M2_INNER // M2_GROUPS
M2_BC = M2_GROUPS * M2_N
M2_CONV = M2_INNER + 2 * M2_BC
CONV_W = 4
EPS = 1e-6
CHUNK = 64
SUB = 8
TRI = 16
LANES = 128
STEP_ROWS = 16
VMEM_LIMIT = 56 * 1024 * 1024

A_HQ, A_HF, A_HI, A_HG, A_GQKV = 0, 1024, 2048, 3072, 4096
B_GG, B_MZ, B_MXBC = 0, 1024, 3072
S_BETA, S_A, S_DT = 0, 8, 16
O_HG, O_GDN, O_M2 = 0, HG_W, HG_W + GDN_W


def _cparams(n_axes):
    return pltpu.CompilerParams(dimension_semantics=("arbitrary",) * n_axes,
                                vmem_limit_bytes=VMEM_LIMIT)


def _sigmoid(x):
    return 1.0 / (1.0 + jnp.exp(-x))


def _silu(x):
    return x * _sigmoid(x)


def _softplus(x):
    return jnp.maximum(x, 0.0) + jnp.log1p(jnp.exp(-jnp.abs(x)))


def _log_sigmoid(x):
    return jnp.minimum(x, 0.0) - jnp.log1p(jnp.exp(-jnp.abs(x)))


def _mxu(a, b, dims):
    return lax.dot_general(a, b, (dims, ((), ())), preferred_element_type=f32)


def _dot(a, b):
    return _mxu(a.astype(bf16), b.astype(bf16), ((1,), (0,)))


def _dot_nt(a, b):
    return _mxu(a.astype(bf16), b.astype(bf16), ((1,), (1,)))


def _dot_tn(a, b):
    return _mxu(a.astype(bf16), b.astype(bf16), ((0,), (0,)))


def _split2(x):
    x1 = x.astype(bf16)
    return x1, (x - x1.astype(f32)).astype(bf16)


def _split3(x):
    x1 = x.astype(bf16)
    r = x - x1.astype(f32)
    x2 = r.astype(bf16)
    return x1, x2, (r - x2.astype(f32)).astype(bf16)


def _dot2(a, b):
    a1, a2 = _split2(a)
    b1, b2 = _split2(b)
    d = lambda x, y: _mxu(x, y, ((1,), (0,)))
    return d(a1, b1) + (d(a1, b2) + d(a2, b1))


def _sel_dot(sel, x):
    return sum(_mxu(sel, t, ((1,), (0,))) for t in _split3(x))


def _dot_sel(x, sel):
    return sum(_mxu(t, sel, ((1,), (0,))) for t in _split3(x))


def _bcast_cols_tn(x_masked, ones):
    return sum(_mxu(t, ones, ((0,), (0,))) for t in _split3(x_masked))


def _tril_incl(n):
    r = lax.broadcasted_iota(jnp.int32, (n, n), 0)
    c = lax.broadcasted_iota(jnp.int32, (n, n), 1)
    return jnp.where(r >= c, 1.0, 0.0).astype(bf16)


def _rms(x, w):
    return x * lax.rsqrt(jnp.mean(x * x, axis=-1, keepdims=True) + EPS) * w


def _extract_col(x, lane_idx):
    lane = lax.broadcasted_iota(jnp.int32, x.shape, 1)
    return jnp.sum(jnp.where(lane == lane_idx, x, 0.0), axis=-1, keepdims=True)


def _row_mask(x, j):
    r = lax.broadcasted_iota(jnp.int32, x.shape, 0)
    return jnp.where(r == j, x, 0.0)


def _layer_spec(block, l, index_fn):
    return pl.BlockSpec((None,) + block, lambda *g: (l,) + index_fn(*g))


def _any_spec():
    return pl.BlockSpec(memory_space=pl.ANY)


def _rmsnorm_kernel(x_ref, w_ref, o_ref):
    o_ref[...] = _rms(x_ref[...], w_ref[...]).astype(o_ref.dtype)


def _rmsnorm(x, w, out_dtype, tr):
    m, d = x.shape
    return pl.pallas_call(
        _rmsnorm_kernel, grid=(m // tr,),
        in_specs=[pl.BlockSpec((tr, d), lambda i: (i, 0)), pl.BlockSpec((1, d), lambda i: (0, 0))],
        out_specs=pl.BlockSpec((tr, d), lambda i: (i, 0)),
        out_shape=jax.ShapeDtypeStruct((m, d), out_dtype),
        compiler_params=_cparams(1), name="rmsnorm")(x, w.reshape(1, d))


def _wdot(x, w):
    return jnp.dot(x, w.astype(bf16), preferred_element_type=f32)


def _wdot_nt(x, wt):
    return lax.dot_general(x, wt.astype(bf16), (((1,), (1,)), ((), ())), preferred_element_type=f32)


HI_ROWS = 64


def _mm_nt_kernel(x_ref, lo_ref, *rest, shift):
    o_ref = rest[-1]
    w = lo_ref[...]
    if shift:
        w = jnp.concatenate([w[shift:], rest[0][0:shift]], axis=0)
    o_ref[...] = _wdot_nt(x_ref[...], w)


def _mm_nt(x, wt, l, row0, n, tm, tn):
    m, k = x.shape
    base = (row0 // tn) * tn
    shift = row0 - base
    assert shift % 8 == 0 and shift <= HI_ROWS and tn % HI_ROWS == 0
    lo0 = base // tn
    hi0 = (base + tn) // HI_ROWS
    step = tn // HI_ROWS
    specs = [pl.BlockSpec((tm, k), lambda i, j: (i, 0)), _layer_spec((tn, k), l, lambda i, j: (lo0 + j, 0))]
    if shift:
        specs.append(_layer_spec((HI_ROWS, k), l, lambda i, j: (hi0 + step * j, 0)))
    return pl.pallas_call(
        functools.partial(_mm_nt_kernel, shift=shift), grid=(m // tm, n // tn), in_specs=specs,
        out_specs=pl.BlockSpec((tm, tn), lambda i, j: (i, j)),
        out_shape=jax.ShapeDtypeStruct((m, n), f32),
        compiler_params=_cparams(2), name="mm_nt")(*([x, wt] + ([wt] if shift else [])))


def _mm_small_kernel(x_ref, w1_ref, w2_ref, w3_ref, o_ref):
    k = w1_ref.shape[1]
    w = jnp.concatenate([w1_ref[...], w2_ref[...], w3_ref[...], jnp.zeros((LANES - 48, k), f32)], axis=0)
    o_ref[...] = _wdot_nt(x_ref[...], w)


def _mm_small(x, wt, l, tm):
    m, k = x.shape
    b1, b2 = 7168 // 16, 13328 // 16
    assert 7168 % 16 == 0 and 13328 % 16 == 0
    return pl.pallas_call(
        _mm_small_kernel, grid=(m // tm,),
        in_specs=[pl.BlockSpec((tm, k), lambda i: (i, 0)), _layer_spec((16, k), l, lambda i: (b1, 0)),
                  _layer_spec((16, k), l, lambda i: (b2, 0)), _layer_spec((16, k), l, lambda i: (b2 + 1, 0))],
        out_specs=pl.BlockSpec((tm, LANES), lambda i: (i, 0)),
        out_shape=jax.ShapeDtypeStruct((m, LANES), f32),
        compiler_params=_cparams(1), name="mm_small")(x, wt, wt, wt)


def _mm_resid_kernel(x_ref, w_ref, r_ref, o_ref):
    o_ref[...] = r_ref[...] + _wdot(x_ref[...], w_ref[...])


def _mm_resid(x, w, l, r, tm, tn):
    m, k = x.shape
    n = w.shape[2]
    return pl.pallas_call(
        _mm_resid_kernel, grid=(m // tm, n // tn),
        in_specs=[pl.BlockSpec((tm, k), lambda i, j: (i, 0)), _layer_spec((k, tn), l, lambda i, j: (0, j)),
                  pl.BlockSpec((tm, tn), lambda i, j: (i, j))],
        out_specs=pl.BlockSpec((tm, tn), lambda i, j: (i, j)),
        out_shape=jax.ShapeDtypeStruct((m, n), f32),
        compiler_params=_cparams(2), name="mm_resid")(x, w, r)


def _mm_swiglu_kernel(x_ref, wg_ref, wu_ref, o_ref):
    x = x_ref[...]
    g = _wdot(x, wg_ref[...])
    u = _wdot(x, wu_ref[...])
    o_ref[...] = (_silu(g) * u).astype(o_ref.dtype)


def _mm_swiglu(x, w, l, tm, tn):
    m, k = x.shape
    n = w.shape[2] // 2
    nb = n // tn
    return pl.pallas_call(
        _mm_swiglu_kernel, grid=(m // tm, nb),
        in_specs=[pl.BlockSpec((tm, k), lambda i, j: (i, 0)), _layer_spec((k, tn), l, lambda i, j: (0, j)),
                  _layer_spec((k, tn), l, lambda i, j: (0, j + nb))],
        out_specs=pl.BlockSpec((tm, tn), lambda i, j: (i, j)),
        out_shape=jax.ShapeDtypeStruct((m, n), bf16),
        compiler_params=_cparams(2), name="mm_swiglu")(x, w, w)


def _merge_kernel(o_ref, w_ref, ga_ref, gb_ref, gc_ref, out_ref):
    a = _wdot(o_ref[:, O_HG:O_GDN], w_ref[O_HG:O_GDN, :])
    b = _wdot(o_ref[:, O_GDN:O_M2], w_ref[O_GDN:O_M2, :])
    c = _wdot(o_ref[:, O_M2:], w_ref[O_M2:, :])
    out = _sigmoid(ga_ref[...]) * a + _sigmoid(gb_ref[...]) * b + _sigmoid(gc_ref[...]) * c
    out_ref[...] = out.astype(out_ref.dtype)


def _merge(o, wb, l, gates, tm, tn):
    m, k = o.shape
    n = wb.shape[2]
    gs = D_MODEL // tn
    return pl.pallas_call(
        _merge_kernel, grid=(m // tm, n // tn),
        in_specs=[pl.BlockSpec((tm, k), lambda i, j: (i, 0)), _layer_spec((k, tn), l, lambda i, j: (0, j)),
                  pl.BlockSpec((tm, tn), lambda i, j: (i, j)),
                  pl.BlockSpec((tm, tn), lambda i, j: (i, gs + j)),
                  pl.BlockSpec((tm, tn), lambda i, j: (i, 2 * gs + j))],
        out_specs=pl.BlockSpec((tm, tn), lambda i, j: (i, j)),
        out_shape=jax.ShapeDtypeStruct((m, n), bf16),
        compiler_params=_cparams(2), name="merge")(o, wb, gates, gates, gates)


def _alias_last_input(n_in, has_buf):
    return {n_in - 1: 0} if has_buf else {}


def _hgrn_inputs(qz, z, par):
    q = _silu(qz)
    a = par[0:1]
    b = par[1:2] + _log_sigmoid(z)
    m = jnp.maximum(a, b)
    logf = m + jnp.log(jnp.exp(a - m) + jnp.exp(b - m))
    k = par[2:3] / (1.0 + jnp.exp(z))
    return q, k, logf


def _hgrn_chunk(q, k, logf, v, st, tril, lane_c, sub_r):
    C = CHUNK
    G = _sel_dot(tril, logf)
    o = _dot_nt(q * jnp.exp(G), st)
    a_rows = []
    for a in range(C // SUB):
        lo = a * SUB
        Ga, qa, ka = G[lo:lo + SUB], q[lo:lo + SUB], k[lo:lo + SUB]
        if a > 0:
            Gs = G[lo - 1:lo]
            qt = qa * jnp.exp(Ga - Gs)
            kt = k * jnp.exp(jnp.minimum(Gs - G, 0.0))
            R = jnp.where(lane_c < lo, _dot_nt(qt, kt), 0.0)
        else:
            R = jnp.zeros((SUB, C), f32)
        for jl in range(SUB):
            e = qa * ka[jl:jl + 1] * jnp.exp(jnp.minimum(Ga - Ga[jl:jl + 1], 0.0))
            c = jnp.sum(e, axis=-1, keepdims=True)
            R = jnp.where(lane_c == lo + jl, jnp.where(sub_r >= jl, c, 0.0), R)
        a_rows.append(R)
    A = jnp.concatenate(a_rows, axis=0)
    o = o + _dot(A, v)
    Gl = G[C - 1:C]
    return o, st * jnp.exp(Gl) + _dot_tn(v, k * jnp.exp(Gl - G))


def _hgrn_prompt_kernel(q_ref, f_ref, i_ref, g_ref, par_ref, *rest, nchunk, nh, has_buf):
    o_ref, s_ref, st_scr = rest[1:] if has_buf else rest
    t = pl.program_id(2)
    C = CHUNK

    @pl.when(t == 0)
    def _():
        st_scr[...] = jnp.zeros_like(st_scr)

    tril = _tril_incl(C)
    lane_c = lax.broadcasted_iota(jnp.int32, (SUB, C), 1)
    sub_r = lax.broadcasted_iota(jnp.int32, (SUB, C), 0)

    def chunk(ci, carry):
        rows = pl.ds(pl.multiple_of(ci * C, C), C)
        for j in range(nh):
            cols = slice(j * LANES, (j + 1) * LANES)
            par = par_ref[:, cols]
            q, k, logf = _hgrn_inputs(q_ref[rows, cols], f_ref[rows, cols], par)
            o, st = _hgrn_chunk(q, k, logf, i_ref[rows, cols], st_scr[j], tril, lane_c, sub_r)
            st_scr[j] = st
            o_ref[rows, cols] = (_rms(o, par[3:4]) * _silu(g_ref[rows, cols])).astype(o_ref.dtype)
        return carry

    lax.fori_loop(0, nchunk, chunk, 0)

    @pl.when(t == pl.num_programs(2) - 1)
    def _():
        for j in range(nh):
            s_ref[0, j] = st_scr[j].T


def _hgrn_prompt(pa, par, l, obuf, n_rows, B, T, tb, nh):
    nt = T // tb
    w = nh * LANES
    cb = lambda off: (lambda b, h, t: (b * nt + t, off // w + h))
    ins = [pa, pa, pa, pa, par] + ([obuf] if obuf is not None else [])
    in_specs = [pl.BlockSpec((tb, w), cb(A_HQ)), pl.BlockSpec((tb, w), cb(A_HF)),
                pl.BlockSpec((tb, w), cb(A_HI)), pl.BlockSpec((tb, w), cb(A_HG)),
                _layer_spec((8, w), l, lambda b, h, t: (0, h))] + ([_any_spec()] if obuf is not None else [])
    return pl.pallas_call(
        functools.partial(_hgrn_prompt_kernel, nchunk=tb // CHUNK, nh=nh, has_buf=obuf is not None),
        grid=(B, HEADS // nh, nt), in_specs=in_specs,
        out_specs=[pl.BlockSpec((tb, w), cb(O_HG)),
                   pl.BlockSpec((1, nh, HDIM, HDIM), lambda b, h, t: (b, h, 0, 0))],
        out_shape=[jax.ShapeDtypeStruct((n_rows, D_MODEL), bf16),
                   jax.ShapeDtypeStruct((B, HEADS, HDIM, HDIM), f32)],
        scratch_shapes=[pltpu.VMEM((nh, HDIM, HDIM), f32)],
        input_output_aliases=_alias_last_input(len(ins), obuf is not None),
        compiler_params=_cparams(3), name="hgrn_prompt")(*ins)


def _hgrn_step_kernel(q_ref, f_ref, i_ref, g_ref, par_ref, s_ref, *rest):
    o_ref, so_ref = rest[-2:]
    R = STEP_ROWS
    par = par_ref[...]
    q, k, logf = _hgrn_inputs(q_ref[...], f_ref[...], par)
    v = i_ref[...]
    ef = jnp.exp(logf)
    ones = jnp.ones((R, LANES), bf16)
    row = lax.broadcasted_iota(jnp.int32, (R, LANES), 0)
    o = jnp.zeros((R, LANES), f32)
    for j in range(R):
        so_ref[j, 0] = s_ref[j, 0] * _bcast_cols_tn(_row_mask(ef, j), ones) + _dot_tn(_row_mask(k, j), v)
    for j in range(R):
        o = jnp.where(row == j, _dot(q, so_ref[j, 0]), o)
    o_ref[...] = (_rms(o, par[3:4]) * _silu(g_ref[...])).astype(o_ref.dtype)


def _hgrn_step(pa, par, l, states, obuf, sbuf, row0, nb):
    R = STEP_ROWS
    rb = row0 // R
    cb = lambda off: (lambda i, h: (rb + i, off // LANES + h))
    st_spec = _layer_spec((R, 1, HDIM, HDIM), l, lambda i, h: (i, h, 0, 0))
    ins = [pa, pa, pa, pa, par, states, obuf] + ([sbuf] if sbuf is not None else [])
    aliases = {6: 0, 7: 1} if sbuf is not None else {6: 0}
    return pl.pallas_call(
        _hgrn_step_kernel, grid=(nb // R, HEADS),
        in_specs=[pl.BlockSpec((R, LANES), cb(A_HQ)), pl.BlockSpec((R, LANES), cb(A_HF)),
                  pl.BlockSpec((R, LANES), cb(A_HI)), pl.BlockSpec((R, LANES), cb(A_HG)),
                  _layer_spec((8, LANES), l, lambda i, h: (0, h)), st_spec, _any_spec()]
        + ([_any_spec()] if sbuf is not None else []),
        out_specs=[pl.BlockSpec((R, LANES), cb(O_HG)), st_spec],
        out_shape=[jax.ShapeDtypeStruct(obuf.shape, bf16), jax.ShapeDtypeStruct(states.shape, f32)],
        input_output_aliases=aliases,
        compiler_params=_cparams(2), name="hgrn_step")(*ins)


def _l2norm(t):
    return t * lax.rsqrt(jnp.sum(t * t, axis=-1, keepdims=True) + EPS)


def _gdn_gates(sm, p2, h):
    beta = _extract_col(_sigmoid(sm), S_BETA + h)
    g_all = -jnp.exp(p2[1:2]) * _softplus(sm + p2[0:1])
    return beta, _extract_col(g_all, S_A + h)


def _conv_block(xp_scr, x_ref, w, tb, first):
    @pl.when(first)
    def _():
        xp_scr[0:8, :] = jnp.zeros((8, xp_scr.shape[1]), f32)

    xp_scr[8:8 + tb, :] = x_ref[...]
    out = xp_scr[5:5 + tb, :] * w[0:1]
    for j in range(1, CONV_W):
        out = out + xp_scr[5 + j:5 + j + tb, :] * w[j:j + 1]
    xp_scr[5:8, :] = xp_scr[tb + 5:tb + 8, :]
    return out


def _solve_unit_lower(N, rhs, r, c):
    eye = jnp.where(r == c, 1.0, 0.0)
    blk = (r // TRI) == (c // TRI)
    Nd = jnp.where(blk, N, 0.0)
    No = jnp.where(blk, 0.0, N)
    D = eye + Nd
    P = Nd
    for _ in range(3):
        P = _dot2(P, P)
        yield
        D = D + _dot2(D, P)
        yield
    M = _dot2(D, No)
    Y = _dot2(D, rhs)
    yield
    M2 = _dot2(M, M)
    yield
    W = eye + M
    W = W + _dot2(W, M2)
    yield
    return _dot2(W, Y)


def _gdn_chunk(q, k, v, beta, g, S, tril, r, c):
    C = CHUNK
    gb = jnp.broadcast_to(g, (C, LANES))
    Gb = _sel_dot(tril, gb)
    Dm = _sel_dot(tril, jnp.where(r > c, gb[:, 0:C], 0.0))
    kb = k * beta
    kk = _dot_nt(kb, k)
    qk = _dot_nt(q, k)
    yield
    dec = jnp.where(r >= c, jnp.exp(jnp.minimum(Dm, 0.0)), 0.0)
    eG = jnp.exp(Gb)
    N = jnp.where(r > c, -kk * dec, 0.0)
    X = yield from _solve_unit_lower(N, jnp.concatenate([v * beta, kb * eG], axis=1), r, c)
    yield
    Vn = X[:, 0:HDIM] - _dot(X[:, HDIM:], S)
    o_state = _dot(q * eG, S)
    yield
    o = o_state + _dot(qk * dec, Vn)
    Gl = Gb[C - 1:C]
    return o, S * jnp.exp(Gl) + _dot_tn(k * jnp.exp(Gl - Gb), Vn)


def _round_robin(gens):
    results = [None] * len(gens)
    live = list(range(len(gens)))
    while live:
        for i in list(live):
            try:
                next(gens[i])
            except StopIteration as stop:
                results[i] = stop.value
                live.remove(i)
    return results


def _gdn_prompt_kernel(q_ref, k_ref, v_ref, g_ref, sm_ref, wq_ref, wk_ref, wv_ref, p2_ref, obuf_ref,
                       o_ref, s_ref, xq, xk, xv, cq, ck, cv, s_scr, *, nchunk, tb, nh):
    hb = pl.program_id(1)
    t = pl.program_id(2)
    C = CHUNK
    first = t == 0

    @pl.when(first)
    def _():
        s_scr[...] = jnp.zeros_like(s_scr)

    cq[...] = _silu(_conv_block(xq, q_ref, wq_ref[...], tb, first))
    ck[...] = _silu(_conv_block(xk, k_ref, wk_ref[...], tb, first))
    cv[...] = _silu(_conv_block(xv, v_ref, wv_ref[...], tb, first))

    p2 = p2_ref[...]
    tril = _tril_incl(C)
    r = lax.broadcasted_iota(jnp.int32, (C, C), 0)
    c = lax.broadcasted_iota(jnp.int32, (C, C), 1)

    def chunk(ci, carry):
        rows = pl.ds(pl.multiple_of(ci * C, C), C)
        sm = sm_ref[rows, :]
        gens = []
        for j in range(nh):
            cols = slice(j * LANES, (j + 1) * LANES)
            q = _l2norm(cq[rows, cols]) * (HDIM ** -0.5)
            k = _l2norm(ck[rows, cols])
            beta, g = _gdn_gates(sm, p2, hb * nh + j)
            gens.append(_gdn_chunk(q, k, cv[rows, cols], beta, g, s_scr[j], tril, r, c))
        for j, (o, S) in enumerate(_round_robin(gens)):
            cols = slice(j * LANES, (j + 1) * LANES)
            s_scr[j] = S
            o_ref[rows, cols] = (_rms(o, p2[2:3]) * _silu(g_ref[rows, cols])).astype(o_ref.dtype)
        return carry

    lax.fori_loop(0, nchunk, chunk, 0)

    @pl.when(t == pl.num_programs(2) - 1)
    def _():
        s_ref[0] = s_scr[...]


def _gdn_prompt(pa, pb, ps, pconv, p2, l, obuf, B, T, tb, nh):
    nt = T // tb
    w = nh * LANES
    cb = lambda off: (lambda b, h, t: (b * nt + t, off // w + h))
    wb = lambda off: (lambda b, h, t: (0, off // w + h))
    return pl.pallas_call(
        functools.partial(_gdn_prompt_kernel, nchunk=tb // CHUNK, tb=tb, nh=nh),
        grid=(B, HEADS // nh, nt),
        in_specs=[pl.BlockSpec((tb, w), cb(A_GQKV)), pl.BlockSpec((tb, w), cb(A_GQKV + GDN_W)),
                  pl.BlockSpec((tb, w), cb(A_GQKV + 2 * GDN_W)), pl.BlockSpec((tb, w), cb(B_GG)),
                  pl.BlockSpec((tb, LANES), lambda b, h, t: (b * nt + t, 0)),
                  _layer_spec((8, w), l, wb(0)), _layer_spec((8, w), l, wb(GDN_W)),
                  _layer_spec((8, w), l, wb(2 * GDN_W)),
                  _layer_spec((8, LANES), l, lambda b, h, t: (0, 0)), _any_spec()],
        out_specs=[pl.BlockSpec((tb, w), cb(O_GDN)),
                   pl.BlockSpec((1, nh, HDIM, HDIM), lambda b, h, t: (b, h, 0, 0))],
        out_shape=[jax.ShapeDtypeStruct(obuf.shape, bf16),
                   jax.ShapeDtypeStruct((B, HEADS, HDIM, HDIM), f32)],
        scratch_shapes=[pltpu.VMEM((tb + 8, w), f32)] * 3 + [pltpu.VMEM((tb, w), f32)] * 3
        + [pltpu.VMEM((nh, HDIM, HDIM), f32)],
        input_output_aliases={9: 0},
        compiler_params=_cparams(3), name="gdn_prompt")(pa, pa, pa, pb, ps, pconv, pconv, pconv, p2, obuf)


def _conv_step(x, b0, b1, b2, w):
    return b0 * w[0:1] + b1 * w[1:2] + b2 * w[2:3] + x * w[3:4]


def _gdn_step_kernel(q_ref, k_ref, v_ref, g_ref, sm_ref, q0, q1, q2, k0, k1, k2, v0, v1, v2,
                     wq_ref, wk_ref, wv_ref, p2_ref, s_ref, *rest):
    o_ref, so_ref = rest[-2:]
    R = STEP_ROWS
    h = pl.program_id(1)
    p2 = p2_ref[...]
    q = _l2norm(_silu(_conv_step(q_ref[...], q0[...], q1[...], q2[...], wq_ref[...]))) * (HDIM ** -0.5)
    k = _l2norm(_silu(_conv_step(k_ref[...], k0[...], k1[...], k2[...], wk_ref[...])))
    v = _silu(_conv_step(v_ref[...], v0[...], v1[...], v2[...], wv_ref[...]))
    beta, g = _gdn_gates(sm_ref[...], p2, h)
    eg = jnp.exp(jnp.broadcast_to(g, (R, LANES)))
    row = lax.broadcasted_iota(jnp.int32, (R, LANES), 0)
    qs = jnp.zeros((R, LANES), f32)
    ks = jnp.zeros((R, LANES), f32)
    for j in range(R):
        s = s_ref[j, 0]
        qs = jnp.where(row == j, _dot(q, s), qs)
        ks = jnp.where(row == j, _dot(k, s), ks)
    vn = beta * v - (beta * eg) * ks
    o = eg * qs + jnp.sum(q * k, axis=-1, keepdims=True) * vn
    for j in range(R):
        so_ref[j, 0] = (s_ref[j, 0] * jnp.broadcast_to(eg[j:j + 1, 0:1], (HDIM, HDIM))
                        + _dot_tn(_row_mask(k, j), vn))
    o_ref[...] = (_rms(o, p2[2:3]) * _silu(g_ref[...])).astype(o_ref.dtype)


def _gdn_step(pa, pb, ps, conv_t, pconv, p2, l, states, obuf, sbuf, row0, nb):
    R = STEP_ROWS
    rb = row0 // R
    cb = lambda off: (lambda i, h: (rb + i, off // LANES + h))
    wb = lambda off: (lambda i, h: (0, off // LANES + h))
    tap = lambda off, j: pl.BlockSpec((None, None, R, LANES), lambda i, h: (l, j, i, off // LANES + h))
    taps = [tap(off, j) for off in (0, GDN_W, 2 * GDN_W) for j in range(CONV_W - 1)]
    st_spec = _layer_spec((R, 1, HDIM, HDIM), l, lambda i, h: (i, h, 0, 0))
    ins = [pa, pa, pa, pb, ps] + [conv_t] * 9 + [pconv, pconv, pconv, p2, states, obuf] + (
        [sbuf] if sbuf is not None else [])
    n = len(ins)
    aliases = {n - 2: 0, n - 1: 1} if sbuf is not None else {n - 1: 0}
    return pl.pallas_call(
        _gdn_step_kernel, grid=(nb // R, HEADS),
        in_specs=[pl.BlockSpec((R, LANES), cb(A_GQKV)), pl.BlockSpec((R, LANES), cb(A_GQKV + GDN_W)),
                  pl.BlockSpec((R, LANES), cb(A_GQKV + 2 * GDN_W)), pl.BlockSpec((R, LANES), cb(B_GG)),
                  pl.BlockSpec((R, LANES), lambda i, h: (rb + i, 0))] + taps
        + [_layer_spec((8, LANES), l, wb(0)), _layer_spec((8, LANES), l, wb(GDN_W)),
           _layer_spec((8, LANES), l, wb(2 * GDN_W)), _layer_spec((8, LANES), l, lambda i, h: (0, 0)),
           st_spec, _any_spec()] + ([_any_spec()] if sbuf is not None else []),
        out_specs=[pl.BlockSpec((R, LANES), cb(O_GDN)), st_spec],
        out_shape=[jax.ShapeDtypeStruct(obuf.shape, bf16), jax.ShapeDtypeStruct(states.shape, f32)],
        input_output_aliases=aliases,
        compiler_params=_cparams(2), name="gdn_step")(*ins)


def _head_select(g, width):
    lane = lax.broadcasted_iota(jnp.int32, (LANES, M2_R * width), 0)
    col = lax.broadcasted_iota(jnp.int32, (LANES, M2_R * width), 1)
    return jnp.where(lane == S_DT + g * M2_R + col // width, 1.0, 0.0).astype(bf16)


def _ssd_chunk(x, Bm, Cm, sm, px, sel, st, y_view, tril, r, cmod):
    C = CHUNK
    P = M2_P
    dt_raw = _dot_sel(sm, sel)
    CB = _dot_nt(Cm, Bm)
    y_state = _dot(Cm, st)
    yield
    dt = _softplus(dt_raw + px[5:6])
    a = dt * -jnp.exp(px[6:7])
    GW = _sel_dot(tril, a)
    Dm = _sel_dot(tril, jnp.where(r > cmod, a, 0.0))
    yield
    dec = jnp.where(r >= cmod, jnp.exp(jnp.minimum(Dm, 0.0)), 0.0)
    xdt = x * dt
    for h in range(M2_R):
        sl = slice(h * P, (h + 1) * P)
        y_view[:, sl] = _dot(CB * dec[:, sl], xdt[:, sl])
    Gl = GW[C - 1:C]
    st_new = st * jnp.exp(Gl) + _dot_tn(Bm, xdt * jnp.exp(Gl - GW))
    yield
    return y_view[...] + y_state * jnp.exp(GW) + x * px[7:8], st_new


def _ssd_prompt_kernel(z_ref, x_ref, b_ref, c_ref, sm_ref, px_ref, pb_ref, pc_ref, obuf_ref,
                       o_ref, s_ref, xx, xb, xc, cx, cb_s, cc, y_scr, st_scr, *, nchunk, tb, ng):
    gb = pl.program_id(1)
    t = pl.program_id(2)
    C = CHUNK
    first = t == 0

    @pl.when(first)
    def _():
        st_scr[...] = jnp.zeros_like(st_scr)

    cx[...] = _silu(_conv_block(xx, x_ref, px_ref[...], tb, first) + px_ref[4:5, :])
    cb_s[...] = _silu(_conv_block(xb, b_ref, pb_ref[...], tb, first) + pb_ref[4:5, :])
    cc[...] = _silu(_conv_block(xc, c_ref, pc_ref[...], tb, first) + pc_ref[4:5, :])

    tril = _tril_incl(C)
    r = lax.broadcasted_iota(jnp.int32, (C, M2_GW), 0)
    cmod = lax.broadcasted_iota(jnp.int32, (C, M2_GW), 1) % M2_P
    sels = [_head_select(gb * ng + i, M2_P) for i in range(ng)]

    def chunk(ci, carry):
        rows = pl.ds(pl.multiple_of(ci * C, C), C)
        sm = sm_ref[rows, :]
        gens = []
        for i in range(ng):
            xc_ = slice(i * M2_GW, (i + 1) * M2_GW)
            nc = slice(i * M2_N, (i + 1) * M2_N)
            gens.append(_ssd_chunk(cx[rows, xc_], cb_s[rows, nc], cc[rows, nc], sm, px_ref[:, xc_], sels[i],
                                   st_scr[i], y_scr.at[:, xc_], tril, r, cmod))
        for i, (y, st) in enumerate(_round_robin(gens)):
            xc_ = slice(i * M2_GW, (i + 1) * M2_GW)
            st_scr[i] = st
            y = y * _silu(z_ref[rows, xc_])
            o_ref[rows, xc_] = _rms(y, px_ref[8:9, xc_]).astype(o_ref.dtype)
        return carry

    lax.fori_loop(0, nchunk, chunk, 0)

    @pl.when(t == pl.num_programs(2) - 1)
    def _():
        for i in range(ng):
            s_ref[0, i * M2_R:(i + 1) * M2_R] = st_scr[i].T.reshape(M2_R, M2_P, M2_N)


def _ssd_prompt(pb, ps, px, pbc, l, obuf, B, T, tb, ng):
    nt = T // tb
    wx = ng * M2_GW
    wn = ng * M2_N
    xs = lambda off: (lambda b, g, t: (b * nt + t, off // wx + g))
    bc = lambda off: (lambda b, g, t: (b * nt + t, off // wn + g))
    return pl.pallas_call(
        functools.partial(_ssd_prompt_kernel, nchunk=tb // CHUNK, tb=tb, ng=ng),
        grid=(B, M2_GROUPS // ng, nt),
        in_specs=[pl.BlockSpec((tb, wx), xs(B_MZ)), pl.BlockSpec((tb, wx), xs(B_MXBC)),
                  pl.BlockSpec((tb, wn), bc(B_MXBC + M2_INNER)),
                  pl.BlockSpec((tb, wn), bc(B_MXBC + M2_INNER + M2_BC)),
                  pl.BlockSpec((tb, LANES), lambda b, g, t: (b * nt + t, 0)),
                  _layer_spec((16, wx), l, lambda b, g, t: (0, g)),
                  _layer_spec((8, wn), l, lambda b, g, t: (0, g)),
                  _layer_spec((8, wn), l, lambda b, g, t: (0, M2_BC // wn + g)), _any_spec()],
        out_specs=[pl.BlockSpec((tb, wx), xs(O_M2)),
                   pl.BlockSpec((1, ng * M2_R, M2_P, M2_N), lambda b, g, t: (b, g, 0, 0))],
        out_shape=[jax.ShapeDtypeStruct(obuf.shape, bf16),
                   jax.ShapeDtypeStruct((B, M2_HEADS, M2_P, M2_N), f32)],
        scratch_shapes=[pltpu.VMEM((tb + 8, wx), f32), pltpu.VMEM((tb + 8, wn), f32),
                        pltpu.VMEM((tb + 8, wn), f32), pltpu.VMEM((tb, wx), f32),
                        pltpu.VMEM((tb, wn), f32), pltpu.VMEM((tb, wn), f32),
                        pltpu.VMEM((CHUNK, wx), f32), pltpu.VMEM((ng, M2_N, M2_GW), f32)],
        input_output_aliases={8: 0},
        compiler_params=_cparams(3), name="ssd_prompt")(pb, pb, pb, pb, ps, px, pbc, pbc, obuf)


def _ssd_step_kernel(z_ref, x_ref, b_ref, c_ref, sm_ref, x0, x1, x2, b0, b1, b2, c0, c1, c2,
                     px_ref, pb_ref, pc_ref, s_ref, *rest):
    o_ref, so_ref, y_scr = rest[-3:]
    R = STEP_ROWS
    g = pl.program_id(1)
    P = M2_P
    px = px_ref[...]
    pb = pb_ref[...]
    pc = pc_ref[...]
    x = _silu(_conv_step(x_ref[...], x0[...], x1[...], x2[...], px) + px[4:5])
    Bm = _silu(_conv_step(b_ref[...], b0[...], b1[...], b2[...], pb) + pb[4:5])
    Cm = _silu(_conv_step(c_ref[...], c0[...], c1[...], c2[...], pc) + pc[4:5])
    dt = _softplus(_dot_sel(sm_ref[...], _head_select(g, P)) + px[5:6])
    ea = jnp.exp(dt * -jnp.exp(px[6:7]))
    xdt = x * dt
    row = lax.broadcasted_iota(jnp.int32, (R, P), 0)
    for h in range(M2_R):
        sl = slice(h * P, (h + 1) * P)
        yh = jnp.zeros((R, P), f32)
        for j in range(R):
            s = s_ref[j, h]
            yh = jnp.where(row == j, _dot_nt(Cm, s), yh)
            scale = jnp.broadcast_to(ea[j:j + 1, h * P:h * P + 1], (P, M2_N))
            so_ref[j, h] = s * scale + _dot_tn(_row_mask(xdt[:, sl], j), Bm)
        y_scr[:, sl] = yh
    y = y_scr[...] * ea + jnp.sum(Cm * Bm, axis=-1, keepdims=True) * xdt + x * px[7:8]
    y = y * _silu(z_ref[...])
    o_ref[...] = _rms(y, px[8:9]).astype(o_ref.dtype)


def _ssd_step(pb, ps, conv_t, px, pbc, l, states, obuf, sbuf, row0, nb):
    R = STEP_ROWS
    rb = row0 // R
    xs = lambda off: (lambda i, g: (rb + i, off // M2_GW + g))
    bc = lambda off: (lambda i, g: (rb + i, off // LANES + g))
    xtap = lambda j: pl.BlockSpec((None, None, R, M2_GW), lambda i, g: (l, j, i, g))
    btap = lambda off, j: pl.BlockSpec((None, None, R, LANES), lambda i, g: (l, j, i, off // LANES + g))
    taps = ([xtap(j) for j in range(3)] + [btap(M2_INNER, j) for j in range(3)]
            + [btap(M2_INNER + M2_BC, j) for j in range(3)])
    st_spec = _layer_spec((R, M2_R, M2_P, M2_N), l, lambda i, g: (i, g, 0, 0))
    ins = [pb, pb, pb, pb, ps] + [conv_t] * 9 + [px, pbc, pbc, states, obuf] + ([sbuf] if sbuf is not None else [])
    n = len(ins)
    aliases = {n - 2: 0, n - 1: 1} if sbuf is not None else {n - 1: 0}
    return pl.pallas_call(
        _ssd_step_kernel, grid=(nb // R, M2_GROUPS),
        in_specs=[pl.BlockSpec((R, M2_GW), xs(B_MZ)), pl.BlockSpec((R, M2_GW), xs(B_MXBC)),
                  pl.BlockSpec((R, LANES), bc(B_MXBC + M2_INNER)),
                  pl.BlockSpec((R, LANES), bc(B_MXBC + M2_INNER + M2_BC)),
                  pl.BlockSpec((R, LANES), lambda i, g: (rb + i, 0))] + taps
        + [_layer_spec((16, M2_GW), l, lambda i, g: (0, g)),
           _layer_spec((8, LANES), l, lambda i, g: (0, g)),
           _layer_spec((8, LANES), l, lambda i, g: (0, M2_GROUPS + g)),
           st_spec, _any_spec()] + ([_any_spec()] if sbuf is not None else []),
        out_specs=[pl.BlockSpec((R, M2_GW), xs(O_M2)), st_spec],
        out_shape=[jax.ShapeDtypeStruct(obuf.shape, bf16), jax.ShapeDtypeStruct(states.shape, f32)],
        scratch_shapes=[pltpu.VMEM((R, M2_GW), f32)],
        input_output_aliases=aliases,
        compiler_params=_cparams(2), name="ssd_step")(*ins)


def _pad_rows(a, rows):
    return jnp.pad(a, ((0, 0), (0, rows - a.shape[1]), (0, 0)))


def _pack_params(hg_lb, hg_onorm, gdn_conv, gdn_A_log, gdn_dt_bias, gdn_onorm,
                 m2_conv_w, m2_conv_b, m2_dt_bias, m2_A_log, m2_D, m2_norm):
    lb = jnp.cumsum(jax.nn.softmax(hg_lb.astype(f32), axis=0), axis=0)
    lb = lb - lb[0]
    hg_par = _pad_rows(jnp.stack([jnp.log(lb), jnp.log1p(-lb), 1.0 - lb,
                                  jnp.tile(hg_onorm, (1, HEADS))], axis=1), 8)
    gdn_pconv = _pad_rows(gdn_conv, 8)
    lane_pad = lambda a: jnp.pad(a, ((0, 0), (S_A, LANES - S_A - HEADS)))
    gdn_p2 = _pad_rows(jnp.stack([lane_pad(gdn_dt_bias), lane_pad(gdn_A_log), gdn_onorm], axis=1), 8)
    rep = lambda a: jnp.repeat(a, M2_P, axis=1)
    ssd_px = _pad_rows(jnp.concatenate(
        [m2_conv_w[:, :, :M2_INNER],
         jnp.stack([m2_conv_b[:, :M2_INNER], rep(m2_dt_bias), rep(m2_A_log), rep(m2_D), m2_norm], axis=1)],
        axis=1), 16)
    ssd_pbc = _pad_rows(jnp.concatenate([m2_conv_w[:, :, M2_INNER:], m2_conv_b[:, None, M2_INNER:]], axis=1), 8)
    return hg_par, gdn_pconv, gdn_p2, ssd_px, ssd_pbc


def _trunk(x_all, n_prompt, B, T, states, w, *, tm, tm_ffn, tr, tb, nh):
    st_hg, st_gdn, st_gc, st_ssm, st_sc = states
    n_rows = x_all.shape[0]
    nb = n_rows - n_prompt
    hg_par, gdn_pconv, gdn_p2, ssd_px, ssd_pbc = w["packed"]
    w_in_t = jnp.swapaxes(w["w_in"], 1, 2)
    gc_t = jnp.swapaxes(st_gc, 1, 2)
    sc_t = jnp.swapaxes(st_sc, 1, 2)
    outs = {k: [] for k in ("p_hg", "p_gdn", "p_gc", "p_ssm", "p_sc", "s_gc", "s_sc")}
    s_hg = s_gdn = s_ssm = None
    x = x_all
    last3 = lambda p, c0, c1: jnp.stack([lax.slice(p, (b * T + T - (CONV_W - 1), c0), (b * T + T, c1))
                                         for b in range(B)])
    for l in range(DEPTH):
        h = _rmsnorm(x, w["mix_norm"][l], bf16, tr)
        pa = _mm_nt(h, w_in_t, l, 0, 7168, tm, 256)
        pb = _mm_nt(h, w_in_t, l, 7184, 6144, tm, 256)
        pc = _mm_nt(h, w_in_t, l, 13360, 3 * D_MODEL, tm, 256)
        ps = _mm_small(h, w_in_t, l, tm)
        o, s1p = _hgrn_prompt(pa, hg_par, l, None, n_rows, B, T, tb, nh)
        o, s2p = _gdn_prompt(pa, pb, ps, gdn_pconv, gdn_p2, l, o, B, T, tb, nh)
        o, s3p = _ssd_prompt(pb, ps, ssd_px, ssd_pbc, l, o, B, T, tb // 2, 2)
        o, s_hg = _hgrn_step(pa, hg_par, l, st_hg, o, s_hg, n_prompt, nb)
        o, s_gdn = _gdn_step(pa, pb, ps, gc_t, gdn_pconv, gdn_p2, l, st_gdn, o, s_gdn, n_prompt, nb)
        o, s_ssm = _ssd_step(pb, ps, sc_t, ssd_px, ssd_pbc, l, st_ssm, o, s_ssm, n_prompt, nb)
        merged = _merge(o, w["w_branch"], l, pc, tm_ffn, 512)
        x = _mm_resid(merged, w["w_out"], l, x, tm_ffn, 512)
        h2 = _rmsnorm(x, w["ffn_norm"][l], bf16, tr)
        act = _mm_swiglu(h2, w["w_ffn_in"], l, tm, 256)
        x = _mm_resid(act, w["w_ffn_out"], l, x, tm_ffn, 256)
        gq0, gq1 = A_GQKV, A_GQKV + 3 * GDN_W
        mx0, mx1 = B_MXBC, B_MXBC + M2_CONV
        outs["p_hg"].append(s1p)
        outs["p_gdn"].append(s2p)
        outs["p_gc"].append(last3(pa, gq0, gq1))
        outs["p_ssm"].append(s3p)
        outs["p_sc"].append(last3(pb, mx0, mx1))
        outs["s_gc"].append(jnp.concatenate(
            [st_gc[l][:, 1:], lax.slice(pa, (n_prompt, gq0), (n_rows, gq1))[:, None]], axis=1))
        outs["s_sc"].append(jnp.concatenate(
            [st_sc[l][:, 1:], lax.slice(pb, (n_prompt, mx0), (n_rows, mx1))[:, None]], axis=1))
    y = _rmsnorm(x, w["final_norm"], f32, tr)
    o = {k: jnp.stack(v) for k, v in outs.items()}
    o.update(s_hg=s_hg, s_gdn=s_gdn, s_ssm=s_ssm)
    return y, o


def kernel(x_prompt, x_sample, state_hgrn, state_gdn, state_gdn_conv, state_ssm, state_ssm_conv, mix_norm, w_in, hg_lb, hg_onorm, gdn_conv, gdn_A_log, gdn_dt_bias, gdn_onorm, m2_conv_w, m2_conv_b, m2_dt_bias, m2_A_log, m2_D, m2_norm, w_branch, w_out, ffn_norm, w_ffn_in, w_ffn_out, final_norm):
    B, T, D = x_prompt.shape
    nb = x_sample.shape[0]
    w = {"mix_norm": mix_norm, "ffn_norm": ffn_norm, "final_norm": final_norm,
         "w_in": w_in, "w_branch": w_branch, "w_out": w_out,
         "w_ffn_in": w_ffn_in, "w_ffn_out": w_ffn_out.astype(bf16),
         "packed": _pack_params(hg_lb, hg_onorm, gdn_conv, gdn_A_log, gdn_dt_bias, gdn_onorm,
                                m2_conv_w, m2_conv_b, m2_dt_bias, m2_A_log, m2_D, m2_norm)}
    x_all = jnp.concatenate([x_prompt.reshape(B * T, D), x_sample.reshape(nb, D)], axis=0)
    states = (state_hgrn, state_gdn, state_gdn_conv, state_ssm, state_ssm_conv)
    y, o = _trunk(x_all, B * T, B, T, states, w, tm=1664, tm_ffn=832, tr=208, tb=512, nh=8)
    return (y[:B * T].reshape(B, T, D), y[B * T:].reshape(nb, 1, D),
            o["p_hg"], o["p_gdn"], o["p_gc"], o["p_ssm"], o["p_sc"],
            o["s_hg"], o["s_gdn"], o["s_gc"], o["s_ssm"], o["s_sc"])
```
